```python
import jax
import jax.numpy as jnp
from jax import lax
import numpy as np

D_MODEL = 1024
BATCH = 2
SEQ = 16384
DEPTH = 2

GRID_W = 64
CTX_LEN = 256
N_EVEN = (DEPTH + 1) // 2
N_ODD = DEPTH // 2
DEEPNORM_ALPHA = (2 * DEPTH) ** 0.25
DEEPNORM_BETA = (8 * DEPTH) ** -0.25
LN_EPS = 1e-6
RMS_EPS = 1e-6

LRU_WIDTH = D_MODEL
LRU_HEADS = 16
LRU_HEAD_DIM = LRU_WIDTH // LRU_HEADS
CONV_W = 4
LRU_C = 8.0
FNET_GROUPS = 4
FNET_WIDTH = D_MODEL // 2
FNET_GROUP_DIM = FNET_WIDTH // FNET_GROUPS
MIX0_WIDTH = LRU_WIDTH + FNET_WIDTH
IN0_WIDTH = 2 * MIX0_WIDTH

MLA_HEADS = 16
Q_LORA = 256
KV_LORA = 128
QK_NOPE = 64
QK_ROPE = 32
V_DIM = 64
QK_DIM = QK_NOPE + QK_ROPE
MLA_WIDTH = MLA_HEADS * V_DIM
IN1_WIDTH = Q_LORA + KV_LORA + QK_ROPE + MLA_WIDTH
ROPE_PAIRS = QK_ROPE // 4
ROPE_THETA = 10000.0
ATTN_SCALE = QK_DIM ** -0.5
Q_BLOCK = 128

kernel_name = 'hybrid_rglru_fnet_mla_dit'


def layer_norm(x, g, b):
    xf = x.astype(jnp.float32)
    mu = xf.mean(-1, keepdims=True)
    var = jnp.square(xf - mu).mean(-1, keepdims=True)
    return ((xf - mu) * lax.rsqrt(var + LN_EPS)).astype(x.dtype) * g + b


def rms_norm(x, g):
    xf = x.astype(jnp.float32)
    return (xf * lax.rsqrt(jnp.mean(xf * xf, -1, keepdims=True) + RMS_EPS)).astype(x.dtype) * g


def adaln(cond, w, b):
    return jnp.split(jax.nn.silu(cond) @ w + b, 3, axis=-1)


def centred_dwconv(x, w, b):
    length = x.shape[1]
    left = (CONV_W - 1) // 2
    xp = jnp.pad(x, ((0, 0), (left, CONV_W - 1 - left), (0, 0)))
    return sum(xp[:, k:k + length] * w[k] for k in range(CONV_W)) + b


def rglru_coeffs(xc, gate_w, gate_b, lam):
    b_, length, _ = xc.shape
    xh = xc.reshape(b_, length, LRU_HEADS, LRU_HEAD_DIM)
    g = jnp.einsum('blhi,ghij->gblhj', xh, gate_w).reshape(2, b_, length, LRU_WIDTH)
    g = (g + gate_b[:, None, None, :]).astype(jnp.float32)
    r = jax.nn.sigmoid(g[0])
    i = jax.nn.sigmoid(g[1])
    log_a = -LRU_C * r * jax.nn.softplus(-lam.astype(jnp.float32))
    a = jnp.exp(log_a)
    b = jnp.sqrt(-jnp.expm1(2.0 * log_a)) * (i * xc.astype(jnp.float32))
    return a, b


def linear_scan(a, b, h0):
    def combine(e1, e2):
        a1, b1 = e1
        a2, b2 = e2
        return a1 * a2, a2 * b1 + b2
    a_cum, b_cum = lax.associative_scan(combine, (a, b), axis=1)
    if h0 is None:
        return b_cum
    return a_cum * h0[:, None, :] + b_cum


def rglru_bidirectional(x_ctx, x_lat, gate_w, gate_b, lam, ctx_out):
    y_lat, y_ctx = [], []
    for d in range(2):
        rev = d == 1
        xc = jnp.flip(x_ctx, axis=1) if rev else x_ctx
        xl = jnp.flip(x_lat, axis=1) if rev else x_lat
        a_c, b_c = rglru_coeffs(xc, gate_w[d], gate_b[d], lam[d])
        h_c = linear_scan(a_c, b_c, None)
        a_l, b_l = rglru_coeffs(xl, gate_w[d], gate_b[d], lam[d])
        h_l = linear_scan(a_l, b_l, h_c[:, -1])
        y_lat.append(jnp.flip(h_l, axis=1) if rev else h_l)
        y_ctx.append(jnp.flip(h_c, axis=1) if rev else h_c)
    out_lat = (y_lat[0] + y_lat[1]).astype(x_lat.dtype)
    out_ctx = (y_ctx[0] + y_ctx[1]).astype(x_ctx.dtype) if ctx_out else None
    return out_lat, out_ctx


def fourier_mix(u, w, b):
    b_, length, _ = u.shape
    ug = u.reshape(b_, length, FNET_GROUPS, FNET_GROUP_DIM).astype(jnp.float32)
    f = jnp.fft.fft2(ug, axes=(1, 3), norm='ortho').real.astype(u.dtype)
    return jnp.einsum('blgi,gij->blgj', f, w).reshape(b_, length, FNET_WIDTH) + b


def recurrent_fourier_mixer(u_lat, u_ctx, w_in, conv_w, conv_b, gate_w, gate_b, lam,
                            fnet_w, fnet_b, w_out, ctx_out):
    z_lat = u_lat @ w_in
    z_ctx = u_ctx @ w_in
    xl_lat = centred_dwconv(z_lat[..., :LRU_WIDTH], conv_w, conv_b)
    xl_ctx = centred_dwconv(z_ctx[..., :LRU_WIDTH], conv_w, conv_b)
    r_lat, r_ctx = rglru_bidirectional(xl_ctx, xl_lat, gate_w, gate_b, lam, ctx_out)

    def merge(z, r):
        f = fourier_mix(z[..., LRU_WIDTH:MIX0_WIDTH], fnet_w, fnet_b)
        return (jnp.concatenate([r, f], axis=-1) * jax.nn.silu(z[..., MIX0_WIDTH:])) @ w_out

    y_lat = merge(z_lat, r_lat)
    y_ctx = merge(z_ctx, r_ctx) if ctx_out else None
    return y_lat, y_ctx


def axial_rope_tables(rows):
    inv = ROPE_THETA ** (-jnp.arange(ROPE_PAIRS, dtype=jnp.float32) / ROPE_PAIRS)
    row = jnp.broadcast_to(jnp.arange(rows, dtype=jnp.float32)[:, None], (rows, GRID_W)).reshape(-1)
    col = jnp.broadcast_to(jnp.arange(GRID_W, dtype=jnp.float32)[None, :], (rows, GRID_W)).reshape(-1)
    ang = jnp.stack([row[:, None] * inv, col[:, None] * inv], axis=1)
    return jnp.cos(ang), jnp.sin(ang)


def apply_axial_rope(x, cos, sin):
    xs = x.reshape(*x.shape[:-1], 2, 2, ROPE_PAIRS)
    x1, x2 = xs[..., 0, :], xs[..., 1, :]
    c = cos[:, None].astype(x.dtype)
    s = sin[:, None].astype(x.dtype)
    return jnp.stack([x1 * c - x2 * s, x2 * c + x1 * s], axis=-2).reshape(x.shape)


def attend(q, k, v):
    s = jnp.einsum('bqhd,bkhd->bhqk', q, k, preferred_element_type=jnp.float32) * ATTN_SCALE
    p = jax.nn.softmax(s, axis=-1).astype(v.dtype)
    return jnp.einsum('bhqk,bkhd->bqhd', p, v)


def blocked_attention(q, k, v):
    b_, s_, h_, dq = q.shape
    nb = s_ // Q_BLOCK
    qb = jnp.moveaxis(q.reshape(b_, nb, Q_BLOCK, h_, dq), 1, 0)
    ob = lax.map(lambda qi: attend(qi, k, v), qb)
    return jnp.moveaxis(ob, 0, 1).reshape(b_, s_, h_, V_DIM)


def mla_mixer(u_lat, u_ctx, w_in, q_norm_g, kv_norm_g, w_uq, w_ukv, w_out, cos, sin, ctx_out):
    bounds = [Q_LORA, Q_LORA + KV_LORA, Q_LORA + KV_LORA + QK_ROPE]
    qc_l, kvc_l, kr_l, g_l = jnp.split(u_lat @ w_in, bounds, axis=-1)
    qc_c, kvc_c, kr_c, g_c = jnp.split(u_ctx @ w_in, bounds, axis=-1)

    def expand_q(qc):
        b_, length, _ = qc.shape
        return (rms_norm(qc, q_norm_g) @ w_uq).reshape(b_, length, MLA_HEADS, QK_DIM)

    def expand_kv(kvc, k_rope):
        b_, length, _ = kvc.shape
        kv = (rms_norm(kvc, kv_norm_g) @ w_ukv).reshape(b_, length, MLA_HEADS, QK_NOPE + V_DIM)
        k_rope = jnp.broadcast_to(k_rope[:, :, None, :], (b_, length, MLA_HEADS, QK_ROPE))
        return jnp.concatenate([kv[..., :QK_NOPE], k_rope], axis=-1), kv[..., QK_NOPE:]

    b_, s_, _ = u_lat.shape
    q_l = expand_q(qc_l)
    q_l = jnp.concatenate([q_l[..., :QK_NOPE], apply_axial_rope(q_l[..., QK_NOPE:], cos, sin)], axis=-1)
    kr_l = apply_axial_rope(kr_l[:, :, None, :], cos, sin)[:, :, 0]
    k_l, v_l = expand_kv(kvc_l, kr_l)
    k_c, v_c = expand_kv(kvc_c, kr_c)
    k_all = jnp.concatenate([k_c, k_l], axis=1)
    v_all = jnp.concatenate([v_c, v_l], axis=1)
    o_l = blocked_attention(q_l, k_all, v_all).reshape(b_, s_, MLA_WIDTH)
    y_lat = (o_l * jax.nn.silu(g_l)) @ w_out
    y_ctx = None
    if ctx_out:
        o_c = attend(expand_q(qc_c), k_c, v_c).reshape(b_, u_ctx.shape[1], MLA_WIDTH)
        y_ctx = (o_c * jax.nn.silu(g_c)) @ w_out
    return y_lat, y_ctx


def setup_inputs(seed: int = 0) -> dict:
    key = jax.random.key(seed)
    keys = iter(jax.random.split(key, 40))

    def normal(shape, scale):
        return jax.random.normal(next(keys), shape, jnp.float32) * scale

    D = D_MODEL
    a0 = jax.random.uniform(next(keys), (N_EVEN, 2, LRU_WIDTH), jnp.float32, 0.9, 0.999)
    s = a0 ** (1.0 / LRU_C)
    lru_lambda = jnp.log(s) - jnp.log1p(-s)
    return {
        'x': normal((BATCH, SEQ, D), 1.0),
        'c': normal((BATCH, D), 1.0),
        'ctx': normal((BATCH, CTX_LEN, D), 1.0),
        'c_ctx': normal((D,), 1.0),
        'ada_w': normal((DEPTH, D, 3 * D), D ** -0.5),
        'ada_b': normal((DEPTH, 3 * D), 0.02),
        'ln_g': 1.0 + normal((DEPTH, D), 0.02),
        'ln_b': normal((DEPTH, D), 0.02),
        'w_in_rf': normal((N_EVEN, D, IN0_WIDTH), D ** -0.5),
        'conv_w': normal((N_EVEN, CONV_W, LRU_WIDTH), CONV_W ** -0.5),
        'conv_b': normal((N_EVEN, LRU_WIDTH), 0.02),
        'lru_gate_w': normal((N_EVEN, 2, 2, LRU_HEADS, LRU_HEAD_DIM, LRU_HEAD_DIM), LRU_HEAD_DIM ** -0.5),
        'lru_gate_b': normal((N_EVEN, 2, 2, LRU_WIDTH), 0.02),
        'lru_lambda': lru_lambda,
        'fnet_w': normal((N_EVEN, FNET_GROUPS, FNET_GROUP_DIM, FNET_GROUP_DIM), FNET_GROUP_DIM ** -0.5),
        'fnet_b': normal((N_EVEN, FNET_WIDTH), 0.02),
        'w_out_rf': normal((N_EVEN, MIX0_WIDTH, D), MIX0_WIDTH ** -0.5 * DEEPNORM_BETA),
        'w_in_mla': normal((N_ODD, D, IN1_WIDTH), D ** -0.5),
        'q_norm_g': 1.0 + normal((N_ODD, Q_LORA), 0.02),
        'kv_norm_g': 1.0 + normal((N_ODD, KV_LORA), 0.02),
        'w_uq': normal((N_ODD, Q_LORA, MLA_HEADS * QK_DIM), Q_LORA ** -0.5),
        'w_ukv': normal((N_ODD, KV_LORA, MLA_HEADS * (QK_NOPE + V_DIM)), KV_LORA ** -0.5),
        'w_out_mla': normal((N_ODD, MLA_WIDTH, D), MLA_WIDTH ** -0.5 * DEEPNORM_BETA),
    }


def reference(x, c, ctx, c_ctx, ada_w, ada_b, ln_g, ln_b,
              w_in_rf, conv_w, conv_b, lru_gate_w, lru_gate_b, lru_lambda, fnet_w, fnet_b, w_out_rf,
              w_in_mla, q_norm_g, kv_norm_g, w_uq, w_ukv, w_out_mla):
    rows = x.shape[1] // GRID_W
    cos, sin = axial_rope_tables(rows)
    h_lat, h_ctx = x, ctx
    for layer in range(DEPTH):
        j = layer // 2
        ctx_out = layer < DEPTH - 1
        shift, scale, gate = adaln(c, ada_w[layer], ada_b[layer])
        shift_c, scale_c, gate_c = adaln(c_ctx, ada_w[layer], ada_b[layer])
        u_lat = h_lat * (1.0 + scale[:, None, :]) + shift[:, None, :]
        u_ctx = h_ctx * (1.0 + scale_c) + shift_c
        if layer % 2 == 0:
            y_lat, y_ctx = recurrent_fourier_mixer(
                u_lat, u_ctx, w_in_rf[j], conv_w[j], conv_b[j], lru_gate_w[j], lru_gate_b[j],
                lru_lambda[j], fnet_w[j], fnet_b[j], w_out_rf[j], ctx_out)
        else:
            y_lat, y_ctx = mla_mixer(
                u_lat, u_ctx, w_in_mla[j], q_norm_g[j], kv_norm_g[j], w_uq[j], w_ukv[j],
                w_out_mla[j], cos, sin, ctx_out)
        h_lat = layer_norm(DEEPNORM_ALPHA * h_lat + gate[:, None, :] * y_lat, ln_g[layer], ln_b[layer])
        if ctx_out:
            h_ctx = layer_norm(DEEPNORM_ALPHA * h_ctx + gate_c * y_ctx, ln_g[layer], ln_b[layer])
    return h_lat
```

```python
import functools
import math

import numpy as np
import jax
import jax.numpy as jnp
from jax import lax
from jax.experimental import pallas as pl
from jax.experimental.pallas import tpu as pltpu

F32 = jnp.float32
BF16 = jnp.bfloat16

DEPTH = 2
GRID_W = 64
DEEPNORM_ALPHA = (2 * DEPTH) ** 0.25
LN_EPS = 1e-6
RMS_EPS = 1e-6
LRU_HEADS = 16
LRU_HEAD_DIM = 64
LRU_GROUP = 256
CONV_W = 4
LRU_C = 8.0
FNET_GROUPS = 4
FNET_GROUP_DIM = 128
MLA_HEADS = 16
Q_LORA = 256
KV_LORA = 128
QK_NOPE = 64
QK_ROPE = 32
V_DIM = 64
QK_DIM = QK_NOPE + QK_ROPE
ROPE_PAIRS = QK_ROPE // 4
ROPE_THETA = 10000.0
ATTN_SCALE = QK_DIM ** -0.5
LOG2E = 1.4426950408889634

LANES = 128
SUBLANES = 8
BF16_ROWS = 16
VMEM_LIMIT = 56 * 1024 * 1024

HEAD_PAD = LANES
FFT_N2 = 128
FFT_PITCH = FFT_N2 + SUBLANES
NEG_BIG = -1e30


def _sigmoid(x):
    return 0.5 * (jnp.tanh(0.5 * x) + 1.0)


def _silu(x):
    return x * _sigmoid(x)


def _cparams(sem):
    return pltpu.CompilerParams(dimension_semantics=sem, vmem_limit_bytes=VMEM_LIMIT)


def _adaln_kernel(cond_ref, w_ref, b_ref, o_ref):
    c = cond_ref[...]
    o_ref[...] = jnp.dot(_silu(c), w_ref[...], preferred_element_type=F32,
                         precision=lax.Precision.HIGHEST) + b_ref[...]


def _adaln(cond, ada_w, ada_b):
    depth, d, n = ada_w.shape
    rows = cond.shape[0]
    tn = 768
    return pl.pallas_call(
        _adaln_kernel,
        grid=(depth, n // tn),
        in_specs=[
            pl.BlockSpec((rows, d), lambda l, j: (0, 0)),
            pl.BlockSpec((None, d, tn), lambda l, j: (l, 0, j)),
            pl.BlockSpec((None, 1, tn), lambda l, j: (l, 0, j)),
        ],
        out_specs=pl.BlockSpec((None, rows, tn), lambda l, j: (l, 0, j)),
        out_shape=jax.ShapeDtypeStruct((depth, rows, n), F32),
        compiler_params=_cparams(("parallel", "parallel")),
        name="adaln",
    )(cond, ada_w, ada_b.reshape(depth, 1, n))


def _in0_kernel(x_ref, mod_ref, w_ref, zl_ref, zf_ref, sg_ref, *, lru_w, fn_w):
    shift = mod_ref[0:1, :]
    scale = mod_ref[1:2, :]
    u = (x_ref[...] * (1.0 + scale) + shift).astype(BF16)
    zl_ref[...] = jnp.dot(u, w_ref[:, :lru_w], preferred_element_type=F32).astype(BF16)
    zf_ref[...] = jnp.dot(u, w_ref[:, lru_w:lru_w + fn_w], preferred_element_type=F32).astype(BF16)
    g0 = lru_w + fn_w
    for c0 in range(0, lru_w + fn_w, 512):
        g = jnp.dot(u, w_ref[:, g0 + c0:g0 + c0 + 512], preferred_element_type=F32)
        sg_ref[:, c0:c0 + 512] = _silu(g).astype(BF16)


def _in0(x, mod, w_bf, tm):
    b_, l_, d = x.shape
    lru_w, fn_w = d, d // 2
    mix = lru_w + fn_w
    return pl.pallas_call(
        functools.partial(_in0_kernel, lru_w=lru_w, fn_w=fn_w),
        grid=(b_, l_ // tm),
        in_specs=[
            pl.BlockSpec((None, tm, d), lambda b, i: (b, i, 0)),
            pl.BlockSpec((None, 3, d), lambda b, i: (b, 0, 0)),
            pl.BlockSpec((d, 2 * mix), lambda b, i: (0, 0)),
        ],
        out_specs=[
            pl.BlockSpec((None, tm, lru_w), lambda b, i: (b, i, 0)),
            pl.BlockSpec((None, tm, fn_w), lambda b, i: (b, i, 0)),
            pl.BlockSpec((None, tm, mix), lambda b, i: (b, i, 0)),
        ],
        out_shape=[
            jax.ShapeDtypeStruct((b_, l_, lru_w), BF16),
            jax.ShapeDtypeStruct((b_, l_, fn_w), BF16),
            jax.ShapeDtypeStruct((b_, l_, mix), BF16),
        ],
        compiler_params=_cparams(("parallel", "parallel")),
        name="in_proj0",
    )(x, mod, w_bf)


def _scan_kernel(z_ref, zp_ref, zn_ref, cw_ref, cb_ref, wg_ref, gb_ref, lam_ref, h0_ref,
                 h_ref, hfin_ref, zz_sc, a_sc, b_sc, hs_sc, carry_sc, *, reverse, t_rows):
    i = pl.program_id(1)
    nt = pl.num_programs(1)
    tt = (nt - 1 - i) if reverse else i
    halo = BF16_ROWS

    @pl.when(i == 0)
    def _():
        carry_sc[...] = jnp.broadcast_to(h0_ref[...], carry_sc.shape)

    zz_sc[0:halo, :] = jnp.where(tt > 0, zp_ref[...].astype(F32), 0.0)
    zz_sc[halo:halo + t_rows, :] = z_ref[...].astype(F32)
    zz_sc[halo + t_rows:halo + t_rows + halo, :] = jnp.where(tt < nt - 1, zn_ref[...].astype(F32), 0.0)

    width = z_ref.shape[-1]
    nl = -lam_ref[...]
    c_all = LRU_C * (jnp.maximum(nl, 0.0) + jnp.log(1.0 + jnp.exp(-jnp.abs(nl))))
    for j in range(width // LRU_GROUP):
        cs = slice(j * LRU_GROUP, (j + 1) * LRU_GROUP)
        xc = cb_ref[:, cs]
        for k in range(CONV_W):
            xc = xc + cw_ref[k:k + 1, cs] * zz_sc[halo - 1 + k:halo - 1 + k + t_rows, cs]
        g = jnp.dot(xc.astype(BF16), wg_ref[j], preferred_element_type=F32)
        r = _sigmoid(g[:, :LRU_GROUP] + gb_ref[0:1, cs])
        ig = _sigmoid(g[:, LRU_GROUP:] + gb_ref[1:2, cs])
        log_a = -c_all[:, cs] * r
        a = jnp.exp(log_a)
        one_m_a2 = jnp.tanh(-log_a) * (1.0 + a * a)
        a_sc[:, cs] = a
        b_sc[:, cs] = jnp.sqrt(one_m_a2) * (ig * xc)

    def group(gi, h):
        base = (t_rows - SUBLANES - gi * SUBLANES) if reverse else gi * SUBLANES
        base = pl.multiple_of(base, SUBLANES)
        for t in range(SUBLANES):
            row = base + ((SUBLANES - 1 - t) if reverse else t)
            h = a_sc[pl.ds(row, 1), :] * h + b_sc[pl.ds(row, 1), :]
            hs_sc[pl.ds(row, 1), :] = h
        return h

    h_last = lax.fori_loop(0, t_rows // SUBLANES, group, carry_sc[0:1, :])
    carry_sc[0:1, :] = h_last
    h_ref[...] = hs_sc[...].astype(BF16)

    @pl.when(i == nt - 1)
    def _():
        hfin_ref[...] = h_last


def _lru_scan(zl, conv_w, conv_b, wg_d, gb_d, lam_d, h0_d, *, reverse, t_rows):
    b_, l_, w = zl.shape
    nt = l_ // t_rows
    hb = t_rows // BF16_ROWS
    n_hb = l_ // BF16_ROWS

    def tile(i):
        return (nt - 1 - i) if reverse else i

    return pl.pallas_call(
        functools.partial(_scan_kernel, reverse=reverse, t_rows=t_rows),
        grid=(b_, nt),
        in_specs=[
            pl.BlockSpec((None, t_rows, w), lambda b, i: (b, tile(i), 0)),
            pl.BlockSpec((None, BF16_ROWS, w), lambda b, i: (b, jnp.maximum(tile(i) * hb - 1, 0), 0)),
            pl.BlockSpec((None, BF16_ROWS, w), lambda b, i: (b, jnp.minimum((tile(i) + 1) * hb, n_hb - 1), 0)),
            pl.BlockSpec((CONV_W, w), lambda b, i: (0, 0)),
            pl.BlockSpec((1, w), lambda b, i: (0, 0)),
            pl.BlockSpec((w // LRU_GROUP, LRU_GROUP, 2 * LRU_GROUP), lambda b, i: (0, 0, 0)),
            pl.BlockSpec((2, w), lambda b, i: (0, 0)),
            pl.BlockSpec((1, w), lambda b, i: (0, 0)),
            pl.BlockSpec((None, 1, w), lambda b, i: (b, 0, 0)),
        ],
        out_specs=[
            pl.BlockSpec((None, t_rows, w), lambda b, i: (b, tile(i), 0)),
            pl.BlockSpec((None, 1, w), lambda b, i: (b, 0, 0)),
        ],
        out_shape=[
            jax.ShapeDtypeStruct((b_, l_, w), BF16),
            jax.ShapeDtypeStruct((b_, 1, w), F32),
        ],
        scratch_shapes=[
            pltpu.VMEM((t_rows + 2 * BF16_ROWS, w), F32),
            pltpu.VMEM((t_rows, w), F32),
            pltpu.VMEM((t_rows, w), F32),
            pltpu.VMEM((t_rows, w), F32),
            pltpu.VMEM((SUBLANES, w), F32),
        ],
        compiler_params=_cparams(("parallel", "arbitrary")),
        name="lru_scan_bwd" if reverse else "lru_scan_fwd",
    )(zl, zl, zl, conv_w, conv_b, wg_d, gb_d, lam_d, h0_d)


def _dft_cos_sin(n):
    k = np.arange(n, dtype=np.int64)
    ang = 2.0 * np.pi * ((k[:, None] * k[None, :]) % n).astype(np.float64) / n
    return np.cos(ang), np.sin(ang)


def _fnet_lat_kernel(x_ref, fc_ref, m1_ref, c2s2_ref, twa_ref, twb_ref, fw_ref, fb_ref, o_ref, s_sc,
                     *, n1, norm):
    n2 = FFT_N2
    pitch = FFT_PITCH
    gd = x_ref.shape[-1]

    def stage0(i, carry):
        r0 = pl.multiple_of(i * n2, n2)
        u = jnp.dot(x_ref[pl.ds(r0, n2), :], fc_ref[...], preferred_element_type=F32)
        d0 = pl.multiple_of(i * pitch, SUBLANES)
        s_sc[0, pl.ds(d0, n2), :] = u[:, :gd]
        s_sc[1, pl.ds(d0, n2), :] = u[:, gd:]
        return carry

    lax.fori_loop(0, n1, stage0, 0)

    def stage1(a, carry):
        ta_r = twa_ref[0, a]
        ta_i = twa_ref[1, a]
        for b in range(SUBLANES):
            col = a * SUBLANES + b
            g_r = s_sc[0, pl.ds(col, n1, stride=pitch), :]
            g_i = s_sc[1, pl.ds(col, n1, stride=pitch), :]
            g = jnp.concatenate([g_r, g_i], axis=0).astype(BF16)
            y = jnp.dot(m1_ref[...], g, preferred_element_type=F32)
            y_r, y_i = y[:n1], y[n1:]
            tb_r = twb_ref[0, b]
            tb_i = twb_ref[1, b]
            c = ta_r * tb_r - ta_i * tb_i
            s = -(ta_r * tb_i + ta_i * tb_r)
            s_sc[0, pl.ds(col, n1, stride=pitch), :] = y_r * c + y_i * s
            s_sc[1, pl.ds(col, n1, stride=pitch), :] = y_i * c - y_r * s
        return carry

    lax.fori_loop(0, n2 // SUBLANES, stage1, 0)

    def stage2(k1, carry):
        d0 = pl.multiple_of(k1 * pitch, SUBLANES)
        z = jnp.concatenate([s_sc[0, pl.ds(d0, n2), :], s_sc[1, pl.ds(d0, n2), :]], axis=0)
        xr = jnp.dot(c2s2_ref[...], z.astype(BF16), preferred_element_type=F32) * norm
        out = jnp.dot(xr.astype(BF16), fw_ref[...], preferred_element_type=F32) + fb_ref[...]
        o_ref[pl.ds(k1, n2, stride=n1), :] = out
        return carry

    lax.fori_loop(0, n1, stage2, 0)


def _fnet_lat(zf, fnet_w_bf, fnet_b):
    b_, l_, fw = zf.shape
    gd = FNET_GROUP_DIM
    groups = fw // gd
    n2 = FFT_N2
    n1 = l_ // n2
    assert n1 * n2 == l_ and n1 % SUBLANES == 0
    cc, sc = _dft_cos_sin(gd)
    fc = jnp.asarray(np.concatenate([cc, -sc], axis=1), BF16)
    c1, s1 = _dft_cos_sin(n1)
    m1 = jnp.asarray(np.block([[c1, s1], [-s1, c1]]), BF16)
    c2, s2 = _dft_cos_sin(n2)
    c2s2 = jnp.asarray(np.concatenate([c2, s2], axis=1), BF16)
    k1 = np.arange(n1, dtype=np.float64)[None, :, None]
    ang_a = 2.0 * np.pi * (np.arange(n2 // SUBLANES, dtype=np.float64) * SUBLANES)[:, None, None] * k1 / l_
    ang_b = 2.0 * np.pi * np.arange(SUBLANES, dtype=np.float64)[:, None, None] * k1 / l_
    ones = np.ones((1, 1, gd))
    twa = jnp.asarray(np.stack([np.cos(ang_a) * ones, -np.sin(ang_a) * ones]), F32)
    twb = jnp.asarray(np.stack([np.cos(ang_b) * ones, -np.sin(ang_b) * ones]), F32)
    norm = 1.0 / math.sqrt(l_ * gd)
    const2 = lambda b, g: (0, 0)
    const4 = lambda b, g: (0, 0, 0, 0)
    return pl.pallas_call(
        functools.partial(_fnet_lat_kernel, n1=n1, norm=norm),
        grid=(b_, groups),
        in_specs=[
            pl.BlockSpec((None, l_, gd), lambda b, g: (b, 0, g)),
            pl.BlockSpec(fc.shape, const2),
            pl.BlockSpec(m1.shape, const2),
            pl.BlockSpec(c2s2.shape, const2),
            pl.BlockSpec(twa.shape, const4),
            pl.BlockSpec(twb.shape, const4),
            pl.BlockSpec((None, gd, gd), lambda b, g: (g, 0, 0)),
            pl.BlockSpec((None, 1, gd), lambda b, g: (g, 0, 0)),
        ],
        out_specs=pl.BlockSpec((None, l_, gd), lambda b, g: (b, 0, g)),
        out_shape=jax.ShapeDtypeStruct((b_, l_, fw), F32),
        scratch_shapes=[pltpu.VMEM((2, n1 * FFT_PITCH, gd), F32)],
        compiler_params=_cparams(("parallel", "parallel")),
        name="fnet_lat",
    )(zf, fc, m1, c2s2, twa, twb, fnet_w_bf, fnet_b.reshape(groups, 1, gd))


def _fnet_ctx_kernel(x_ref, fc_ref, cs_ref, fw_ref, fb_ref, o_ref, *, norm):
    gd = x_ref.shape[-1]
    u = jnp.dot(x_ref[...], fc_ref[...], preferred_element_type=F32)
    z = jnp.concatenate([u[:, :gd], u[:, gd:]], axis=0).astype(BF16)
    xr = jnp.dot(cs_ref[...], z, preferred_element_type=F32) * norm
    o_ref[...] = jnp.dot(xr.astype(BF16), fw_ref[...], preferred_element_type=F32) + fb_ref[...]


def _fnet_ctx(zf, fnet_w_bf, fnet_b):
    b_, l_, fw = zf.shape
    gd = FNET_GROUP_DIM
    groups = fw // gd
    cc, sc = _dft_cos_sin(gd)
    fc = jnp.asarray(np.concatenate([cc, -sc], axis=1), BF16)
    cl, sl = _dft_cos_sin(l_)
    cs = jnp.asarray(np.concatenate([cl, sl], axis=1), BF16)
    norm = 1.0 / math.sqrt(l_ * gd)
    const2 = lambda b, g: (0, 0)
    return pl.pallas_call(
        functools.partial(_fnet_ctx_kernel, norm=norm),
        grid=(b_, groups),
        in_specs=[
            pl.BlockSpec((None, l_, gd), lambda b, g: (b, 0, g)),
            pl.BlockSpec(fc.shape, const2),
            pl.BlockSpec(cs.shape, const2),
            pl.BlockSpec((None, gd, gd), lambda b, g: (g, 0, 0)),
            pl.BlockSpec((None, 1, gd), lambda b, g: (g, 0, 0)),
        ],
        out_specs=pl.BlockSpec((None, l_, gd), lambda b, g: (b, 0, g)),
        out_shape=jax.ShapeDtypeStruct((b_, l_, fw), F32),
        compiler_params=_cparams(("parallel", "parallel")),
        name="fnet_ctx",
    )(zf, fc, cs, fnet_w_bf, fnet_b.reshape(groups, 1, gd))


def _deepnorm(resid, gate, y, g, b):
    v = DEEPNORM_ALPHA * resid + gate * y
    mu = jnp.mean(v, axis=-1, keepdims=True)
    vc = v - mu
    var = jnp.mean(vc * vc, axis=-1, keepdims=True)
    return vc * lax.rsqrt(var + LN_EPS) * g + b


def _out0_kernel(hf_ref, hb_ref, f_ref, sg_ref, x_ref, mod_ref, w_ref, lg_ref, lb_ref, o_ref, *, lru_w):
    r = hf_ref[...].astype(F32) + hb_ref[...].astype(F32)
    m_l = (r * sg_ref[:, :lru_w].astype(F32)).astype(BF16)
    m_f = (f_ref[...] * sg_ref[:, lru_w:].astype(F32)).astype(BF16)
    y = jnp.dot(m_l, w_ref[:lru_w, :], preferred_element_type=F32)
    y = y + jnp.dot(m_f, w_ref[lru_w:, :], preferred_element_type=F32)
    o_ref[...] = _deepnorm(x_ref[...], mod_ref[2:3, :], y, lg_ref[...], lb_ref[...])


def _out0(hf, hb, f, sg, x, mod, w_bf, ln_g, ln_b, tm):
    b_, l_, d = x.shape
    lru_w = hf.shape[-1]
    fn_w = f.shape[-1]
    mix = lru_w + fn_w
    row = lambda b, i: (b, i, 0)
    const2 = lambda b, i: (0, 0)
    return pl.pallas_call(
        functools.partial(_out0_kernel, lru_w=lru_w),
        grid=(b_, l_ // tm),
        in_specs=[
            pl.BlockSpec((None, tm, lru_w), row),
            pl.BlockSpec((None, tm, lru_w), row),
            pl.BlockSpec((None, tm, fn_w), row),
            pl.BlockSpec((None, tm, mix), row),
            pl.BlockSpec((None, tm, d), row),
            pl.BlockSpec((None, 3, d), lambda b, i: (b, 0, 0)),
            pl.BlockSpec((mix, d), const2),
            pl.BlockSpec((1, d), const2),
            pl.BlockSpec((1, d), const2),
        ],
        out_specs=pl.BlockSpec((None, tm, d), row),
        out_shape=jax.ShapeDtypeStruct((b_, l_, d), F32),
        compiler_params=_cparams(("parallel", "parallel")),
        name="out_proj0",
    )(hf, hb, f, sg, x, mod, w_bf, ln_g, ln_b)


def _out1_kernel(o_ref_in, sg_ref, x_ref, mod_ref, w_ref, lg_ref, lb_ref, o_ref):
    m = (o_ref_in[...].astype(F32) * sg_ref[...].astype(F32)).astype(BF16)
    y = jnp.dot(m, w_ref[...], preferred_element_type=F32)
    o_ref[...] = _deepnorm(x_ref[...], mod_ref[2:3, :], y, lg_ref[...], lb_ref[...])


def _out1(o, sg, x, mod, w_bf, ln_g, ln_b, tm):
    b_, l_, d = x.shape
    wdt = o.shape[-1]
    row = lambda b, i: (b, i, 0)
    const2 = lambda b, i: (0, 0)
    return pl.pallas_call(
        _out1_kernel,
        grid=(b_, l_ // tm),
        in_specs=[
            pl.BlockSpec((None, tm, wdt), row),
            pl.BlockSpec((None, tm, wdt), row),
            pl.BlockSpec((None, tm, d), row),
            pl.BlockSpec((None, 3, d), lambda b, i: (b, 0, 0)),
            pl.BlockSpec((wdt, d), const2),
            pl.BlockSpec((1, d), const2),
            pl.BlockSpec((1, d), const2),
        ],
        out_specs=pl.BlockSpec((None, tm, d), row),
        out_shape=jax.ShapeDtypeStruct((b_, l_, d), F32),
        compiler_params=_cparams(("parallel", "parallel")),
        name="out_proj1",
    )(o, sg, x, mod, w_bf, ln_g, ln_b)


def _rms(x, g):
    return x * lax.rsqrt(jnp.mean(x * x, axis=-1, keepdims=True) + RMS_EPS) * g


def _in1_kernel(*refs, rope, want_q):
    if want_q:
        (x_ref, mod_ref, w_ref, qg_ref, kg_ref, wq_ref, wqs_ref, wk_ref, wv_ref, cos_ref, sin_ref,
         q_ref, k_ref, v_ref, sg_ref) = refs
    else:
        (x_ref, mod_ref, w_ref, kg_ref, wk_ref, wv_ref, k_ref, v_ref) = refs
    shift = mod_ref[0:1, :]
    scale = mod_ref[1:2, :]
    u = (x_ref[...] * (1.0 + scale) + shift).astype(BF16)
    hp = HEAD_PAD
    o_kv = Q_LORA
    o_kr = Q_LORA + KV_LORA
    o_g = o_kr + 2 * hp

    kvn = _rms(jnp.dot(u, w_ref[:, o_kv:o_kv + KV_LORA], preferred_element_type=F32), kg_ref[...]).astype(BF16)
    kr = jnp.dot(u, w_ref[:, o_kr:o_kr + 2 * hp], preferred_element_type=F32)
    if rope:
        cos = cos_ref[...]
        sin = sin_ref[...]
        krope = kr[:, :hp] * cos + kr[:, hp:] * sin
    else:
        krope = kr[:, :hp]
    lane = lax.broadcasted_iota(jnp.int32, (1, hp), 1)
    one_even = (lane == V_DIM).astype(F32)
    one_odd = (lane == 0).astype(F32)
    for h2 in range(MLA_HEADS // 2):
        cs = slice(2 * h2 * hp, (2 * h2 + 2) * hp)
        kn = jnp.dot(kvn, wk_ref[:, cs], preferred_element_type=F32)
        vv = jnp.dot(kvn, wv_ref[:, cs], preferred_element_type=F32)
        k_ref[2 * h2] = (kn[:, :hp] + krope).astype(BF16)
        k_ref[2 * h2 + 1] = (kn[:, hp:] + krope).astype(BF16)
        v_ref[2 * h2] = (vv[:, :hp] + one_even).astype(BF16)
        v_ref[2 * h2 + 1] = (vv[:, hp:] + one_odd).astype(BF16)

    if want_q:
        qn = _rms(jnp.dot(u, w_ref[:, :Q_LORA], preferred_element_type=F32), qg_ref[...]).astype(BF16)
        qscale = ATTN_SCALE * LOG2E
        cq = cos * qscale
        sq = sin * qscale
        for h2 in range(MLA_HEADS // 2):
            cs = slice(2 * h2 * hp, (2 * h2 + 2) * hp)
            z1 = jnp.dot(qn, wq_ref[:, cs], preferred_element_type=F32)
            z2 = jnp.dot(qn, wqs_ref[:, cs], preferred_element_type=F32)
            q_ref[2 * h2] = (z1[:, :hp] * cq + z2[:, :hp] * sq).astype(BF16)
            q_ref[2 * h2 + 1] = (z1[:, hp:] * cq + z2[:, hp:] * sq).astype(BF16)
        gw = sg_ref.shape[-1]
        for c0 in range(0, gw, 512):
            g = jnp.dot(u, w_ref[:, o_g + c0:o_g + c0 + 512], preferred_element_type=F32)
            sg_ref[:, c0:c0 + 512] = _silu(g).astype(BF16)


def _in1(x, mod, w_in_p, qg, kg, wq_p, wqs_p, wk_p, wv_p, cos_t, sin_t, tm, *, want_q):
    b_, l_, d = x.shape
    hp = HEAD_PAD
    nh = MLA_HEADS
    gw = nh * V_DIM
    row = lambda b, i: (b, i, 0)
    const2 = lambda b, i: (0, 0)
    head = lambda b, i: (b, 0, i, 0)
    kv_specs = [pl.BlockSpec((None, nh, tm, hp), head), pl.BlockSpec((None, nh, tm, hp), head)]
    kv_shapes = [jax.ShapeDtypeStruct((b_, nh, l_, hp), BF16)] * 2
    if want_q:
        ins = [x, mod, w_in_p, qg, kg, wq_p, wqs_p, wk_p, wv_p, cos_t, sin_t]
        in_specs = [
            pl.BlockSpec((None, tm, d), row),
            pl.BlockSpec((None, 3, d), lambda b, i: (b, 0, 0)),
            pl.BlockSpec(w_in_p.shape, const2),
            pl.BlockSpec(qg.shape, const2),
            pl.BlockSpec(kg.shape, const2),
            pl.BlockSpec(wq_p.shape, const2),
            pl.BlockSpec(wqs_p.shape, const2),
            pl.BlockSpec(wk_p.shape, const2),
            pl.BlockSpec(wv_p.shape, const2),
            pl.BlockSpec((tm, hp), lambda b, i: (i, 0)),
            pl.BlockSpec((tm, hp), lambda b, i: (i, 0)),
        ]
        out_specs = [pl.BlockSpec((None, nh, tm, hp), head)] + kv_specs + [pl.BlockSpec((None, tm, gw), row)]
        out_shape = [jax.ShapeDtypeStruct((b_, nh, l_, hp), BF16)] + kv_shapes + [
            jax.ShapeDtypeStruct((b_, l_, gw), BF16)]
    else:
        ins = [x, mod, w_in_p, kg, wk_p, wv_p]
        in_specs = [
            pl.BlockSpec((None, tm, d), row),
            pl.BlockSpec((None, 3, d), lambda b, i: (b, 0, 0)),
            pl.BlockSpec(w_in_p.shape, const2),
            pl.BlockSpec(kg.shape, const2),
            pl.BlockSpec(wk_p.shape, const2),
            pl.BlockSpec(wv_p.shape, const2),
        ]
        out_specs = kv_specs
        out_shape = kv_shapes
    return pl.pallas_call(
        functools.partial(_in1_kernel, rope=want_q, want_q=want_q),
        grid=(b_, l_ // tm),
        in_specs=in_specs,
        out_specs=out_specs,
        out_shape=out_shape,
        compiler_params=_cparams(("parallel", "parallel")),
        name="in_proj1" if want_q else "in_proj1_ctx",
    )(*ins)


def _attn_kernel(q_ref, kc_ref, vc_ref, k_ref, v_ref, o_ref, m_sc, acc_sc):
    kj = pl.program_id(3)
    nk = pl.num_programs(3)

    @pl.when(kj == 0)
    def _():
        m_sc[...] = jnp.full(m_sc.shape, NEG_BIG, F32)
        acc_sc[...] = jnp.zeros(acc_sc.shape, F32)

    def update(hh, k, v):
        s = lax.dot_general(q_ref[hh], k, (((1,), (1,)), ((), ())), preferred_element_type=F32)
        m_prev = m_sc[hh]
        m_new = jnp.maximum(m_prev, jnp.max(s, axis=-1, keepdims=True))
        alpha = jnp.exp2(m_prev - m_new)
        p = jnp.concatenate(
            [jnp.exp2(s[:, c:c + LANES] - m_new) for c in range(0, s.shape[1], LANES)], axis=1)
        acc_sc[hh] = alpha * acc_sc[hh] + jnp.dot(p.astype(BF16), v, preferred_element_type=F32)
        m_sc[hh] = m_new

    @pl.when(kj == 0)
    def _():
        for hh in range(2):
            update(hh, kc_ref[hh], vc_ref[hh])

    for hh in range(2):
        update(hh, k_ref[hh], v_ref[hh])

    @pl.when(kj == nk - 1)
    def _():
        lane = lax.broadcasted_iota(jnp.int32, (1, LANES), 1)
        acc0 = acc_sc[0]
        acc1 = acc_sc[1]
        o0 = acc0 * (1.0 / acc0[:, V_DIM:V_DIM + 1])
        o1 = acc1 * (1.0 / acc1[:, 0:1])
        o_ref[...] = jnp.where(lane < V_DIM, o0, o1).astype(BF16)


def _attention(q, k_ctx, v_ctx, k_lat, v_lat, tq, tk):
    b_, nh, s_, hp = q.shape
    c_ = k_ctx.shape[2]
    return pl.pallas_call(
        _attn_kernel,
        grid=(b_, nh // 2, s_ // tq, s_ // tk),
        in_specs=[
            pl.BlockSpec((None, 2, tq, hp), lambda b, h, i, j: (b, h, i, 0)),
            pl.BlockSpec((None, 2, c_, hp), lambda b, h, i, j: (b, h, 0, 0)),
            pl.BlockSpec((None, 2, c_, hp), lambda b, h, i, j: (b, h, 0, 0)),
            pl.BlockSpec((None, 2, tk, hp), lambda b, h, i, j: (b, h, j, 0)),
            pl.BlockSpec((None, 2, tk, hp), lambda b, h, i, j: (b, h, j, 0)),
        ],
        out_specs=pl.BlockSpec((None, tq, hp), lambda b, h, i, j: (b, i, h)),
        out_shape=jax.ShapeDtypeStruct((b_, s_, (nh // 2) * hp), BF16),
        scratch_shapes=[pltpu.VMEM((2, tq, LANES), F32), pltpu.VMEM((2, tq, LANES), F32)],
        compiler_params=_cparams(("parallel", "parallel", "parallel", "arbitrary")),
        name="mla_attention",
    )(q, k_ctx, v_ctx, k_lat, v_lat)


def _gate_blocks(gate_w_d):
    hd = LRU_HEAD_DIM
    per = LRU_GROUP // hd
    ng = LRU_HEADS // per
    w = gate_w_d.reshape(2, ng, per, hd, hd)
    eye = jnp.eye(per, dtype=gate_w_d.dtype)
    blk = jnp.einsum('gnpij,pq->gnpiqj', w, eye).reshape(2, ng, LRU_GROUP, LRU_GROUP)
    return jnp.concatenate([blk[0], blk[1]], axis=-1).astype(BF16)


def _pad_heads(w, per_head, offset=0):
    k = w.shape[0]
    w3 = w.reshape(k, MLA_HEADS, per_head)
    out = jnp.zeros((k, MLA_HEADS, HEAD_PAD), w.dtype)
    out = out.at[:, :, offset:offset + per_head].set(w3)
    return out.reshape(k, MLA_HEADS * HEAD_PAD)


def _rope_swap_idx():
    return np.arange(QK_ROPE) ^ ROPE_PAIRS


def _prep_mla(w_in, w_uq, w_ukv):
    d = w_in.shape[0]
    qc = w_in[:, :Q_LORA]
    kvc = w_in[:, Q_LORA:Q_LORA + KV_LORA]
    kr = w_in[:, Q_LORA + KV_LORA:Q_LORA + KV_LORA + QK_ROPE]
    g = w_in[:, Q_LORA + KV_LORA + QK_ROPE:]
    swap = _rope_swap_idx()
    kr_blk = jnp.zeros((d, HEAD_PAD), w_in.dtype).at[:, QK_NOPE:QK_NOPE + QK_ROPE].set(kr)
    krs_blk = jnp.zeros((d, HEAD_PAD), w_in.dtype).at[:, QK_NOPE:QK_NOPE + QK_ROPE].set(kr[:, swap])
    w_in_p = jnp.concatenate([qc, kvc, kr_blk, krs_blk, g], axis=1).astype(BF16)

    wq3 = w_uq.reshape(Q_LORA, MLA_HEADS, QK_DIM)
    wq_p = _pad_heads(w_uq, QK_DIM).astype(BF16)
    rope_sw = wq3[:, :, QK_NOPE:][:, :, swap]
    wqs = jnp.zeros((Q_LORA, MLA_HEADS, HEAD_PAD), w_uq.dtype).at[:, :, QK_NOPE:QK_NOPE + QK_ROPE].set(rope_sw)
    wqs_p = wqs.reshape(Q_LORA, MLA_HEADS * HEAD_PAD).astype(BF16)

    wkv3 = w_ukv.reshape(KV_LORA, MLA_HEADS, QK_NOPE + V_DIM)
    wk_p = _pad_heads(wkv3[:, :, :QK_NOPE].reshape(KV_LORA, -1), QK_NOPE).astype(BF16)
    wv3 = wkv3[:, :, QK_NOPE:]
    wv = jnp.zeros((KV_LORA, MLA_HEADS // 2, 2, HEAD_PAD), w_ukv.dtype)
    wv3p = wv3.reshape(KV_LORA, MLA_HEADS // 2, 2, V_DIM)
    wv = wv.at[:, :, 0, :V_DIM].set(wv3p[:, :, 0]).at[:, :, 1, V_DIM:].set(wv3p[:, :, 1])
    wv_p = wv.reshape(KV_LORA, MLA_HEADS * HEAD_PAD).astype(BF16)
    return w_in_p, wq_p, wqs_p, wk_p, wv_p


def _rope_tables(s_):
    rows = s_ // GRID_W
    inv = ROPE_THETA ** (-jnp.arange(ROPE_PAIRS, dtype=F32) / ROPE_PAIRS)
    row = jnp.broadcast_to(jnp.arange(rows, dtype=F32)[:, None], (rows, GRID_W)).reshape(-1)
    col = jnp.broadcast_to(jnp.arange(GRID_W, dtype=F32)[None, :], (rows, GRID_W)).reshape(-1)
    a_r = row[:, None] * inv
    a_c = col[:, None] * inv
    cos32 = jnp.concatenate([jnp.cos(a_r), jnp.cos(a_r), jnp.cos(a_c), jnp.cos(a_c)], axis=1)
    sin32 = jnp.concatenate([-jnp.sin(a_r), jnp.sin(a_r), -jnp.sin(a_c), jnp.sin(a_c)], axis=1)
    cos_t = jnp.ones((s_, HEAD_PAD), F32).at[:, QK_NOPE:QK_NOPE + QK_ROPE].set(cos32)
    sin_t = jnp.zeros((s_, HEAD_PAD), F32).at[:, QK_NOPE:QK_NOPE + QK_ROPE].set(sin32)
    return cos_t, sin_t


def kernel(x, c, ctx, c_ctx, ada_w, ada_b, ln_g, ln_b, w_in_rf, conv_w, conv_b, lru_gate_w, lru_gate_b,
           lru_lambda, fnet_w, fnet_b, w_out_rf, w_in_mla, q_norm_g, kv_norm_g, w_uq, w_ukv, w_out_mla):
    b_, s_, d = x.shape
    c_len = ctx.shape[1]
    tm = 512 if s_ % 512 == 0 else s_
    t_scan = 512 if s_ % 512 == 0 else s_

    cond = jnp.zeros((SUBLANES, d), F32).at[:b_].set(c).at[b_].set(c_ctx)
    mod = _adaln(cond, ada_w, ada_b).reshape(DEPTH, SUBLANES, 3, d)
    mod_lat = [mod[l, :b_] for l in range(DEPTH)]
    mod_ctx = [jnp.broadcast_to(mod[l, b_][None], (b_, 3, d)) for l in range(DEPTH)]

    w_in0 = w_in_rf[0].astype(BF16)
    w_out0 = w_out_rf[0].astype(BF16)
    fw_bf = fnet_w[0].astype(BF16)
    wg = [_gate_blocks(lru_gate_w[0, dd]) for dd in range(2)]
    cb = conv_b[0].reshape(1, -1)

    zl_c, zf_c, sg_c = _in0(ctx, mod_ctx[0], w_in0, c_len)
    zl_l, zf_l, sg_l = _in0(x, mod_lat[0], w_in0, tm)

    h_c, h_l = [], []
    for dd in range(2):
        rev = dd == 1
        lam = lru_lambda[0, dd].reshape(1, -1)
        zero = jnp.zeros((b_, 1, d), F32)
        hc, hc_fin = _lru_scan(zl_c, conv_w[0], cb, wg[dd], lru_gate_b[0, dd], lam, zero,
                               reverse=rev, t_rows=c_len)
        hl, _ = _lru_scan(zl_l, conv_w[0], cb, wg[dd], lru_gate_b[0, dd], lam, hc_fin,
                          reverse=rev, t_rows=t_scan)
        h_c.append(hc)
        h_l.append(hl)

    f_c = _fnet_ctx(zf_c, fw_bf, fnet_b[0])
    f_l = _fnet_lat(zf_l, fw_bf, fnet_b[0])

    lg0, lb0 = ln_g[0].reshape(1, d), ln_b[0].reshape(1, d)
    h1_c = _out0(h_c[0], h_c[1], f_c, sg_c, ctx, mod_ctx[0], w_out0, lg0, lb0, c_len)
    h1_l = _out0(h_l[0], h_l[1], f_l, sg_l, x, mod_lat[0], w_out0, lg0, lb0, tm)

    w_in_p, wq_p, wqs_p, wk_p, wv_p = _prep_mla(w_in_mla[0], w_uq[0], w_ukv[0])
    qg = q_norm_g[0].reshape(1, -1)
    kg = kv_norm_g[0].reshape(1, -1)
    cos_t, sin_t = _rope_tables(s_)

    k_c, v_c = _in1(h1_c, mod_ctx[1], w_in_p, qg, kg, wq_p, wqs_p, wk_p, wv_p, cos_t, sin_t, c_len,
                    want_q=False)
    q_l, k_l, v_l, sg1 = _in1(h1_l, mod_lat[1], w_in_p, qg, kg, wq_p, wqs_p, wk_p, wv_p, cos_t, sin_t, tm,
                              want_q=True)

    tq = 512 if s_ % 512 == 0 else s_
    tk = 1024 if s_ % 1024 == 0 else s_
    o = _attention(q_l, k_c, v_c, k_l, v_l, tq, tk)

    lg1, lb1 = ln_g[1].reshape(1, d), ln_b[1].reshape(1, d)
    return _out1(o, sg1, h1_l, mod_lat[1], w_out_mla[0].astype(BF16), lg1, lb1, tm)
```

```python
import functools
import math

import numpy as np
import jax
import jax.numpy as jnp
from jax import lax
from jax.experimental import pallas as pl
from jax.experimental.pallas import tpu as pltpu

F32 = jnp.float32
BF16 = jnp.bfloat16

DEPTH = 2
GRID_W = 64
DEEPNORM_ALPHA = (2 * DEPTH) ** 0.25
LN_EPS = 1e-6
RMS_EPS = 1e-6
LRU_HEADS = 16
LRU_HEAD_DIM = 64
LRU_GROUP = 256
CONV_W = 4
LRU_C = 8.0
FNET_GROUPS = 4
FNET_GROUP_DIM = 128
MLA_HEADS = 16
Q_LORA = 256
KV_LORA = 128
QK_NOPE = 64
QK_ROPE = 32
V_DIM = 64
QK_DIM = QK_NOPE + QK_ROPE
ROPE_PAIRS = QK_ROPE // 4
ROPE_THETA = 10000.0
ATTN_SCALE = QK_DIM ** -0.5
LOG2E = 1.4426950408889634

LANES = 128
SUBLANES = 8
BF16_ROWS = 16
VMEM_LIMIT = 56 * 1024 * 1024

HEAD_PAD = LANES
FFT_N2 = 128
FFT_PITCH = FFT_N2 + SUBLANES
FFT_BATCH = 4
NEG_BIG = -1e30


def _sigmoid(x):
    return 0.5 * (jnp.tanh(0.5 * x) + 1.0)


def _silu(x):
    return x * _sigmoid(x)


def _cparams(sem):
    return pltpu.CompilerParams(dimension_semantics=sem, vmem_limit_bytes=VMEM_LIMIT)


def _adaln_kernel(cond_ref, w_ref, b_ref, o_ref):
    c = cond_ref[...]
    o_ref[...] = jnp.dot(_silu(c), w_ref[...], preferred_element_type=F32,
                         precision=lax.Precision.HIGHEST) + b_ref[...]


def _adaln(cond, ada_w, ada_b):
    depth, d, n = ada_w.shape
    rows = cond.shape[0]
    tn = 768
    return pl.pallas_call(
        _adaln_kernel,
        grid=(depth, n // tn),
        in_specs=[
            pl.BlockSpec((rows, d), lambda l, j: (0, 0)),
            pl.BlockSpec((None, d, tn), lambda l, j: (l, 0, j)),
            pl.BlockSpec((None, 1, tn), lambda l, j: (l, 0, j)),
        ],
        out_specs=pl.BlockSpec((None, rows, tn), lambda l, j: (l, 0, j)),
        out_shape=jax.ShapeDtypeStruct((depth, rows, n), F32),
        compiler_params=_cparams(("parallel", "parallel")),
        name="adaln",
    )(cond, ada_w, ada_b.reshape(depth, 1, n))


def _in0_kernel(x_ref, mod_ref, w_ref, zl_ref, zf_ref, sg_ref, *, lru_w, fn_w):
    shift = mod_ref[0:1, :]
    scale = mod_ref[1:2, :]
    u = (x_ref[...] * (1.0 + scale) + shift).astype(BF16)
    zl_ref[...] = jnp.dot(u, w_ref[:, :lru_w], preferred_element_type=F32).astype(BF16)
    zf_ref[...] = jnp.dot(u, w_ref[:, lru_w:lru_w + fn_w], preferred_element_type=F32).astype(BF16)
    g0 = lru_w + fn_w
    for c0 in range(0, lru_w + fn_w, 512):
        g = jnp.dot(u, w_ref[:, g0 + c0:g0 + c0 + 512], preferred_element_type=F32)
        sg_ref[:, c0:c0 + 512] = _silu(g).astype(BF16)


def _in0(x, mod, w_bf, tm):
    b_, l_, d = x.shape
    lru_w, fn_w = d, d // 2
    mix = lru_w + fn_w
    return pl.pallas_call(
        functools.partial(_in0_kernel, lru_w=lru_w, fn_w=fn_w),
        grid=(b_, l_ // tm),
        in_specs=[
            pl.BlockSpec((None, tm, d), lambda b, i: (b, i, 0)),
            pl.BlockSpec((None, 3, d), lambda b, i: (b, 0, 0)),
            pl.BlockSpec((d, 2 * mix), lambda b, i: (0, 0)),
        ],
        out_specs=[
            pl.BlockSpec((None, tm, lru_w), lambda b, i: (b, i, 0)),
            pl.BlockSpec((None, tm, fn_w), lambda b, i: (b, i, 0)),
            pl.BlockSpec((None, tm, mix), lambda b, i: (b, i, 0)),
        ],
        out_shape=[
            jax.ShapeDtypeStruct((b_, l_, lru_w), BF16),
            jax.ShapeDtypeStruct((b_, l_, fn_w), BF16),
            jax.ShapeDtypeStruct((b_, l_, mix), BF16),
        ],
        compiler_params=_cparams(("parallel", "parallel")),
        name="in_proj0",
    )(x, mod, w_bf)


def _scan_kernel(z_ref, zp_ref, zn_ref, cw_ref, cb_ref, wg_ref, gb_ref, lam_ref, h0_ref,
                 h_ref, hfin_ref, zz_sc, a_sc, b_sc, hs_sc, carry_sc, *, reverse, t_rows):
    i = pl.program_id(1)
    nt = pl.num_programs(1)
    tt = (nt - 1 - i) if reverse else i
    halo = BF16_ROWS

    @pl.when(i == 0)
    def _():
        carry_sc[...] = jnp.broadcast_to(h0_ref[...], carry_sc.shape)

    zz_sc[0:halo, :] = jnp.where(tt > 0, zp_ref[...].astype(F32), 0.0)
    zz_sc[halo:halo + t_rows, :] = z_ref[...].astype(F32)
    zz_sc[halo + t_rows:halo + t_rows + halo, :] = jnp.where(tt < nt - 1, zn_ref[...].astype(F32), 0.0)

    width = z_ref.shape[-1]
    nl = -lam_ref[...]
    half_c = (0.5 * LRU_C) * (jnp.maximum(nl, 0.0) + jnp.log(1.0 + jnp.exp(-jnp.abs(nl))))
    half_gb = 0.5 * gb_ref[...]
    n_buf = t_rows + 2 * halo
    for j in range(width // LRU_GROUP):
        cs = slice(j * LRU_GROUP, (j + 1) * LRU_GROUP)
        zf = zz_sc[:, cs]
        xc = cb_ref[:, cs] + cw_ref[1:2, cs] * zf[halo:halo + t_rows]
        for k, shift in ((0, 1), (2, n_buf - 1), (3, n_buf - 2)):
            xc = xc + cw_ref[k:k + 1, cs] * pltpu.roll(zf, shift, axis=0)[halo:halo + t_rows]
        g = jnp.dot(xc.astype(BF16), wg_ref[j], preferred_element_type=F32)
        t_r = jnp.tanh(g[:, :LRU_GROUP] + half_gb[0:1, cs])
        t_i = jnp.tanh(g[:, LRU_GROUP:] + half_gb[1:2, cs])
        hc = half_c[:, cs]
        nla = hc * t_r + hc
        a = jnp.exp2(nla * (-LOG2E))
        y = jnp.tanh(nla) * (1.0 + a * a)
        sq = jnp.where(y > 0.0, y * lax.rsqrt(y), 0.0)
        hx = 0.5 * xc
        a_sc[:, cs] = a
        b_sc[:, cs] = sq * (hx * t_i + hx)

    def group(gi, h):
        base = (t_rows - SUBLANES - gi * SUBLANES) if reverse else gi * SUBLANES
        base = pl.multiple_of(base, SUBLANES)
        for t in range(SUBLANES):
            row = base + ((SUBLANES - 1 - t) if reverse else t)
            h = a_sc[pl.ds(row, 1), :] * h + b_sc[pl.ds(row, 1), :]
            hs_sc[pl.ds(row, 1), :] = h
        return h

    h_last = lax.fori_loop(0, t_rows // SUBLANES, group, carry_sc[0:1, :])
    carry_sc[0:1, :] = h_last
    h_ref[...] = hs_sc[...].astype(BF16)

    @pl.when(i == nt - 1)
    def _():
        hfin_ref[...] = h_last


def _lru_scan(zl, conv_w, conv_b, wg_d, gb_d, lam_d, h0_d, *, reverse, t_rows):
    b_, l_, w = zl.shape
    nt = l_ // t_rows
    hb = t_rows // BF16_ROWS
    n_hb = l_ // BF16_ROWS

    def tile(i):
        return (nt - 1 - i) if reverse else i

    return pl.pallas_call(
        functools.partial(_scan_kernel, reverse=reverse, t_rows=t_rows),
        grid=(b_, nt),
        in_specs=[
            pl.BlockSpec((None, t_rows, w), lambda b, i: (b, tile(i), 0)),
            pl.BlockSpec((None, BF16_ROWS, w), lambda b, i: (b, jnp.maximum(tile(i) * hb - 1, 0), 0)),
            pl.BlockSpec((None, BF16_ROWS, w), lambda b, i: (b, jnp.minimum((tile(i) + 1) * hb, n_hb - 1), 0)),
            pl.BlockSpec((CONV_W, w), lambda b, i: (0, 0)),
            pl.BlockSpec((1, w), lambda b, i: (0, 0)),
            pl.BlockSpec((w // LRU_GROUP, LRU_GROUP, 2 * LRU_GROUP), lambda b, i: (0, 0, 0)),
            pl.BlockSpec((2, w), lambda b, i: (0, 0)),
            pl.BlockSpec((1, w), lambda b, i: (0, 0)),
            pl.BlockSpec((None, 1, w), lambda b, i: (b, 0, 0)),
        ],
        out_specs=[
            pl.BlockSpec((None, t_rows, w), lambda b, i: (b, tile(i), 0)),
            pl.BlockSpec((None, 1, w), lambda b, i: (b, 0, 0)),
        ],
        out_shape=[
            jax.ShapeDtypeStruct((b_, l_, w), BF16),
            jax.ShapeDtypeStruct((b_, 1, w), F32),
        ],
        scratch_shapes=[
            pltpu.VMEM((t_rows + 2 * BF16_ROWS, w), F32),
            pltpu.VMEM((t_rows, w), F32),
            pltpu.VMEM((t_rows, w), F32),
            pltpu.VMEM((t_rows, w), F32),
            pltpu.VMEM((SUBLANES, w), F32),
        ],
        compiler_params=_cparams(("parallel", "arbitrary")),
        name="lru_scan_bwd" if reverse else "lru_scan_fwd",
    )(zl, zl, zl, conv_w, conv_b, wg_d, gb_d, lam_d, h0_d)


def _dft_cos_sin(n):
    k = np.arange(n, dtype=np.int64)
    ang = 2.0 * np.pi * ((k[:, None] * k[None, :]) % n).astype(np.float64) / n
    return np.cos(ang), np.sin(ang)


def _mxu_const(a):
    return jnp.asarray(a, F32).astype(BF16)


def _fnet_lat_kernel(x_ref, fc_ref, m1_ref, c2s2_ref, twa_ref, twb_ref, fw_ref, fb_ref, o_ref, s_sc,
                     *, n1, norm):
    n2 = FFT_N2
    pitch = FFT_PITCH
    gd = x_ref.shape[-1]

    nb = FFT_BATCH

    def stage0(i, carry):
        r0 = pl.multiple_of(i * (nb * n2), nb * n2)
        u = jnp.dot(x_ref[pl.ds(r0, nb * n2), :], fc_ref[...], preferred_element_type=F32)
        for j in range(nb):
            d0 = pl.multiple_of((i * nb + j) * pitch, SUBLANES)
            s_sc[0, pl.ds(d0, n2), :] = u[j * n2:(j + 1) * n2, :gd]
            s_sc[1, pl.ds(d0, n2), :] = u[j * n2:(j + 1) * n2, gd:]
        return carry

    lax.fori_loop(0, n1 // nb, stage0, 0)

    def stage1(a, carry):
        ta_r = twa_ref[0, a]
        ta_i = twa_ref[1, a]
        gs = []
        for b in range(SUBLANES):
            col = a * SUBLANES + b
            g_r = s_sc[0, pl.ds(col, n1, stride=pitch), :]
            g_i = s_sc[1, pl.ds(col, n1, stride=pitch), :]
            gs.append(jnp.concatenate([g_r, g_i], axis=0).astype(BF16))
        y_all = jnp.dot(m1_ref[...], jnp.concatenate(gs, axis=1), preferred_element_type=F32)
        for b in range(SUBLANES):
            col = a * SUBLANES + b
            y_r = y_all[:n1, b * gd:(b + 1) * gd]
            y_i = y_all[n1:, b * gd:(b + 1) * gd]
            tb_r = twb_ref[0, b]
            tb_i = twb_ref[1, b]
            c = ta_r * tb_r - ta_i * tb_i
            s = -(ta_r * tb_i + ta_i * tb_r)
            s_sc[0, pl.ds(col, n1, stride=pitch), :] = y_r * c + y_i * s
            s_sc[1, pl.ds(col, n1, stride=pitch), :] = y_i * c - y_r * s
        return carry

    lax.fori_loop(0, n2 // SUBLANES, stage1, 0)

    def stage2(i, carry):
        zs = []
        for j in range(nb):
            d0 = pl.multiple_of((i * nb + j) * pitch, SUBLANES)
            zs.append(jnp.concatenate([s_sc[0, pl.ds(d0, n2), :], s_sc[1, pl.ds(d0, n2), :]],
                                      axis=0).astype(BF16))
        z = jnp.concatenate(zs, axis=1)
        xr = jnp.dot(c2s2_ref[...], z, preferred_element_type=F32) * norm
        xs = jnp.concatenate([xr[:, j * gd:(j + 1) * gd] for j in range(nb)], axis=0).astype(BF16)
        out = jnp.dot(xs, fw_ref[...], preferred_element_type=F32) + fb_ref[...]
        for j in range(nb):
            o_ref[pl.ds(i * nb + j, n2, stride=n1), :] = out[j * n2:(j + 1) * n2, :]
        return carry

    lax.fori_loop(0, n1 // nb, stage2, 0)


def _fnet_lat(zf, fnet_w_bf, fnet_b):
    b_, l_, fw = zf.shape
    gd = FNET_GROUP_DIM
    groups = fw // gd
    n2 = FFT_N2
    n1 = l_ // n2
    assert n1 * n2 == l_ and n1 % SUBLANES == 0
    cc, sc = _dft_cos_sin(gd)
    fc = _mxu_const(np.concatenate([cc, -sc], axis=1))
    c1, s1 = _dft_cos_sin(n1)
    m1 = _mxu_const(np.block([[c1, s1], [-s1, c1]]))
    c2, s2 = _dft_cos_sin(n2)
    c2s2 = _mxu_const(np.concatenate([c2, s2], axis=1))
    k1 = np.arange(n1, dtype=np.float64)[None, :, None]
    ang_a = 2.0 * np.pi * (np.arange(n2 // SUBLANES, dtype=np.float64) * SUBLANES)[:, None, None] * k1 / l_
    ang_b = 2.0 * np.pi * np.arange(SUBLANES, dtype=np.float64)[:, None, None] * k1 / l_
    ones = np.ones((1, 1, gd))
    twa = jnp.asarray(np.stack([np.cos(ang_a) * ones, -np.sin(ang_a) * ones]), F32)
    twb = jnp.asarray(np.stack([np.cos(ang_b) * ones, -np.sin(ang_b) * ones]), F32)
    norm = 1.0 / math.sqrt(l_ * gd)
    const2 = lambda b, g: (0, 0)
    const4 = lambda b, g: (0, 0, 0, 0)
    return pl.pallas_call(
        functools.partial(_fnet_lat_kernel, n1=n1, norm=norm),
        grid=(b_, groups),
        in_specs=[
            pl.BlockSpec((None, l_, gd), lambda b, g: (b, 0, g)),
            pl.BlockSpec(fc.shape, const2),
            pl.BlockSpec(m1.shape, const2),
            pl.BlockSpec(c2s2.shape, const2),
            pl.BlockSpec(twa.shape, const4),
            pl.BlockSpec(twb.shape, const4),
            pl.BlockSpec((None, gd, gd), lambda b, g: (g, 0, 0)),
            pl.BlockSpec((None, 1, gd), lambda b, g: (g, 0, 0)),
        ],
        out_specs=pl.BlockSpec((None, l_, gd), lambda b, g: (b, 0, g)),
        out_shape=jax.ShapeDtypeStruct((b_, l_, fw), F32),
        scratch_shapes=[pltpu.VMEM((2, n1 * FFT_PITCH, gd), F32)],
        compiler_params=_cparams(("parallel", "parallel")),
        name="fnet_lat",
    )(zf, fc, m1, c2s2, twa, twb, fnet_w_bf, fnet_b.reshape(groups, 1, gd))


def _fnet_ctx_kernel(x_ref, fc_ref, cs_ref, fw_ref, fb_ref, o_ref, *, norm):
    gd = x_ref.shape[-1]
    u = jnp.dot(x_ref[...], fc_ref[...], preferred_element_type=F32)
    z = jnp.concatenate([u[:, :gd], u[:, gd:]], axis=0).astype(BF16)
    xr = jnp.dot(cs_ref[...], z, preferred_element_type=F32) * norm
    o_ref[...] = jnp.dot(xr.astype(BF16), fw_ref[...], preferred_element_type=F32) + fb_ref[...]


def _fnet_ctx(zf, fnet_w_bf, fnet_b):
    b_, l_, fw = zf.shape
    gd = FNET_GROUP_DIM
    groups = fw // gd
    cc, sc = _dft_cos_sin(gd)
    fc = _mxu_const(np.concatenate([cc, -sc], axis=1))
    cl, sl = _dft_cos_sin(l_)
    cs = _mxu_const(np.concatenate([cl, sl], axis=1))
    norm = 1.0 / math.sqrt(l_ * gd)
    const2 = lambda b, g: (0, 0)
    return pl.pallas_call(
        functools.partial(_fnet_ctx_kernel, norm=norm),
        grid=(b_, groups),
        in_specs=[
            pl.BlockSpec((None, l_, gd), lambda b, g: (b, 0, g)),
            pl.BlockSpec(fc.shape, const2),
            pl.BlockSpec(cs.shape, const2),
            pl.BlockSpec((None, gd, gd), lambda b, g: (g, 0, 0)),
            pl.BlockSpec((None, 1, gd), lambda b, g: (g, 0, 0)),
        ],
        out_specs=pl.BlockSpec((None, l_, gd), lambda b, g: (b, 0, g)),
        out_shape=jax.ShapeDtypeStruct((b_, l_, fw), F32),
        compiler_params=_cparams(("parallel", "parallel")),
        name="fnet_ctx",
    )(zf, fc, cs, fnet_w_bf, fnet_b.reshape(groups, 1, gd))


def _deepnorm(resid, gate, y, g, b):
    v = DEEPNORM_ALPHA * resid + gate * y
    mu = jnp.mean(v, axis=-1, keepdims=True)
    vc = v - mu
    var = jnp.mean(vc * vc, axis=-1, keepdims=True)
    return vc * lax.rsqrt(var + LN_EPS) * g + b


def _out0_kernel(hf_ref, hb_ref, f_ref, sg_ref, x_ref, mod_ref, w_ref, lg_ref, lb_ref, o_ref, *, lru_w):
    r = hf_ref[...].astype(F32) + hb_ref[...].astype(F32)
    m_l = (r * sg_ref[:, :lru_w].astype(F32)).astype(BF16)
    m_f = (f_ref[...] * sg_ref[:, lru_w:].astype(F32)).astype(BF16)
    y = jnp.dot(m_l, w_ref[:lru_w, :], preferred_element_type=F32)
    y = y + jnp.dot(m_f, w_ref[lru_w:, :], preferred_element_type=F32)
    o_ref[...] = _deepnorm(x_ref[...], mod_ref[2:3, :], y, lg_ref[...], lb_ref[...])


def _out0(hf, hb, f, sg, x, mod, w_bf, ln_g, ln_b, tm):
    b_, l_, d = x.shape
    lru_w = hf.shape[-1]
    fn_w = f.shape[-1]
    mix = lru_w + fn_w
    row = lambda b, i: (b, i, 0)
    const2 = lambda b, i: (0, 0)
    return pl.pallas_call(
        functools.partial(_out0_kernel, lru_w=lru_w),
        grid=(b_, l_ // tm),
        in_specs=[
            pl.BlockSpec((None, tm, lru_w), row),
            pl.BlockSpec((None, tm, lru_w), row),
            pl.BlockSpec((None, tm, fn_w), row),
            pl.BlockSpec((None, tm, mix), row),
            pl.BlockSpec((None, tm, d), row),
            pl.BlockSpec((None, 3, d), lambda b, i: (b, 0, 0)),
            pl.BlockSpec((mix, d), const2),
            pl.BlockSpec((1, d), const2),
            pl.BlockSpec((1, d), const2),
        ],
        out_specs=pl.BlockSpec((None, tm, d), row),
        out_shape=jax.ShapeDtypeStruct((b_, l_, d), F32),
        compiler_params=_cparams(("parallel", "parallel")),
        name="out_proj0",
    )(hf, hb, f, sg, x, mod, w_bf, ln_g, ln_b)


def _out1_kernel(o_ref_in, sg_ref, x_ref, mod_ref, w_ref, lg_ref, lb_ref, o_ref):
    m = (o_ref_in[...].astype(F32) * sg_ref[...].astype(F32)).astype(BF16)
    y = jnp.dot(m, w_ref[...], preferred_element_type=F32)
    o_ref[...] = _deepnorm(x_ref[...], mod_ref[2:3, :], y, lg_ref[...], lb_ref[...])


def _out1(o, sg, x, mod, w_bf, ln_g, ln_b, tm):
    b_, l_, d = x.shape
    wdt = o.shape[-1]
    row = lambda b, i: (b, i, 0)
    const2 = lambda b, i: (0, 0)
    return pl.pallas_call(
        _out1_kernel,
        grid=(b_, l_ // tm),
        in_specs=[
            pl.BlockSpec((None, tm, wdt), row),
            pl.BlockSpec((None, tm, wdt), row),
            pl.BlockSpec((None, tm, d), row),
            pl.BlockSpec((None, 3, d), lambda b, i: (b, 0, 0)),
            pl.BlockSpec((wdt, d), const2),
            pl.BlockSpec((1, d), const2),
            pl.BlockSpec((1, d), const2),
        ],
        out_specs=pl.BlockSpec((None, tm, d), row),
        out_shape=jax.ShapeDtypeStruct((b_, l_, d), F32),
        compiler_params=_cparams(("parallel", "parallel")),
        name="out_proj1",
    )(o, sg, x, mod, w_bf, ln_g, ln_b)


def _rms(x, g):
    return x * lax.rsqrt(jnp.mean(x * x, axis=-1, keepdims=True) + RMS_EPS) * g


def _in1_kernel(*refs, rope, want_q):
    if want_q:
        (x_ref, mod_ref, w_ref, qg_ref, kg_ref, wq_ref, wqs_ref, wk_ref, wv_ref, cos_ref, sin_ref,
         q_ref, k_ref, v_ref, sg_ref) = refs
    else:
        (x_ref, mod_ref, w_ref, kg_ref, wk_ref, wv_ref, k_ref, v_ref) = refs
    shift = mod_ref[0:1, :]
    scale = mod_ref[1:2, :]
    u = (x_ref[...] * (1.0 + scale) + shift).astype(BF16)
    hp = HEAD_PAD
    o_kv = Q_LORA
    o_kr = Q_LORA + KV_LORA
    o_g = o_kr + 2 * hp

    kvn = _rms(jnp.dot(u, w_ref[:, o_kv:o_kv + KV_LORA], preferred_element_type=F32), kg_ref[...]).astype(BF16)
    kr = jnp.dot(u, w_ref[:, o_kr:o_kr + 2 * hp], preferred_element_type=F32)
    if rope:
        cos = cos_ref[...]
        sin = sin_ref[...]
        krope = kr[:, :hp] * cos + kr[:, hp:] * sin
    else:
        krope = kr[:, :hp]
    lane = lax.broadcasted_iota(jnp.int32, (1, hp), 1)
    one_even = (lane == V_DIM).astype(F32)
    one_odd = (lane == 0).astype(F32)
    for h2 in range(MLA_HEADS // 2):
        cs = slice(2 * h2 * hp, (2 * h2 + 2) * hp)
        kn = jnp.dot(kvn, wk_ref[:, cs], preferred_element_type=F32)
        vv = jnp.dot(kvn, wv_ref[:, cs], preferred_element_type=F32)
        k_ref[2 * h2] = (kn[:, :hp] + krope).astype(BF16)
        k_ref[2 * h2 + 1] = (kn[:, hp:] + krope).astype(BF16)
        v_ref[2 * h2] = (vv[:, :hp] + one_even).astype(BF16)
        v_ref[2 * h2 + 1] = (vv[:, hp:] + one_odd).astype(BF16)

    if want_q:
        qn = _rms(jnp.dot(u, w_ref[:, :Q_LORA], preferred_element_type=F32), qg_ref[...]).astype(BF16)
        qscale = ATTN_SCALE * LOG2E
        cq = cos * qscale
        sq = sin * qscale
        for h2 in range(MLA_HEADS // 2):
            cs = slice(2 * h2 * hp, (2 * h2 + 2) * hp)
            z1 = jnp.dot(qn, wq_ref[:, cs], preferred_element_type=F32)
            z2 = jnp.dot(qn, wqs_ref[:, cs], preferred_element_type=F32)
            q_ref[2 * h2] = (z1[:, :hp] * cq + z2[:, :hp] * sq).astype(BF16)
            q_ref[2 * h2 + 1] = (z1[:, hp:] * cq + z2[:, hp:] * sq).astype(BF16)
        gw = sg_ref.shape[-1]
        for c0 in range(0, gw, 512):
            g = jnp.dot(u, w_ref[:, o_g + c0:o_g + c0 + 512], preferred_element_type=F32)
            sg_ref[:, c0:c0 + 512] = _silu(g).astype(BF16)


def _in1(x, mod, w_in_p, qg, kg, wq_p, wqs_p, wk_p, wv_p, cos_t, sin_t, tm, *, want_q):
    b_, l_, d = x.shape
    hp = HEAD_PAD
    nh = MLA_HEADS
    gw = nh * V_DIM
    row = lambda b, i: (b, i, 0)
    const2 = lambda b, i: (0, 0)
    head = lambda b, i: (b, 0, i, 0)
    kv_specs = [pl.BlockSpec((None, nh, tm, hp), head), pl.BlockSpec((None, nh, tm, hp), head)]
    kv_shapes = [jax.ShapeDtypeStruct((b_, nh, l_, hp), BF16)] * 2
    if want_q:
        ins = [x, mod, w_in_p, qg, kg, wq_p, wqs_p, wk_p, wv_p, cos_t, sin_t]
        in_specs = [
            pl.BlockSpec((None, tm, d), row),
            pl.BlockSpec((None, 3, d), lambda b, i: (b, 0, 0)),
            pl.BlockSpec(w_in_p.shape, const2),
            pl.BlockSpec(qg.shape, const2),
            pl.BlockSpec(kg.shape, const2),
            pl.BlockSpec(wq_p.shape, const2),
            pl.BlockSpec(wqs_p.shape, const2),
            pl.BlockSpec(wk_p.shape, const2),
            pl.BlockSpec(wv_p.shape, const2),
            pl.BlockSpec((tm, hp), lambda b, i: (i, 0)),
            pl.BlockSpec((tm, hp), lambda b, i: (i, 0)),
        ]
        out_specs = [pl.BlockSpec((None, nh, tm, hp), head)] + kv_specs + [pl.BlockSpec((None, tm, gw), row)]
        out_shape = [jax.ShapeDtypeStruct((b_, nh, l_, hp), BF16)] + kv_shapes + [
            jax.ShapeDtypeStruct((b_, l_, gw), BF16)]
    else:
        ins = [x, mod, w_in_p, kg, wk_p, wv_p]
        in_specs = [
            pl.BlockSpec((None, tm, d), row),
            pl.BlockSpec((None, 3, d), lambda b, i: (b, 0, 0)),
            pl.BlockSpec(w_in_p.shape, const2),
            pl.BlockSpec(kg.shape, const2),
            pl.BlockSpec(wk_p.shape, const2),
            pl.BlockSpec(wv_p.shape, const2),
        ]
        out_specs = kv_specs
        out_shape = kv_shapes
    return pl.pallas_call(
        functools.partial(_in1_kernel, rope=want_q, want_q=want_q),
        grid=(b_, l_ // tm),
        in_specs=in_specs,
        out_specs=out_specs,
        out_shape=out_shape,
        compiler_params=_cparams(("parallel", "parallel")),
        name="in_proj1" if want_q else "in_proj1_ctx",
    )(*ins)


def _attn_kernel(q_ref, kc_ref, vc_ref, k_ref, v_ref, o_ref, m_sc, acc_sc, *, chunk):
    kj = pl.program_id(3)
    nk = pl.num_programs(3)

    @pl.when(kj == 0)
    def _():
        m_sc[...] = jnp.full(m_sc.shape, NEG_BIG, F32)
        acc_sc[...] = jnp.zeros(acc_sc.shape, F32)

    def pair_update(ks, vs):
        ss = [lax.dot_general(q_ref[hh], ks[hh], (((1,), (1,)), ((), ())), preferred_element_type=F32)
              for hh in range(2)]
        for hh in range(2):
            s = ss[hh]
            m_prev = m_sc[hh]
            m_new = jnp.maximum(m_prev, jnp.max(s, axis=-1, keepdims=True))
            alpha = jnp.exp2(m_prev - m_new)
            p = jnp.concatenate(
                [jnp.exp2(s[:, c:c + LANES] - m_new).astype(BF16) for c in range(0, s.shape[1], LANES)],
                axis=1)
            acc_sc[hh] = alpha * acc_sc[hh] + jnp.dot(p, vs[hh], preferred_element_type=F32)
            m_sc[hh] = m_new

    @pl.when(kj == 0)
    def _():
        pair_update([kc_ref[0], kc_ref[1]], [vc_ref[0], vc_ref[1]])

    def body(ci, carry):
        o = pl.multiple_of(ci * chunk, chunk)
        pair_update([k_ref[0, pl.ds(o, chunk), :], k_ref[1, pl.ds(o, chunk), :]],
                    [v_ref[0, pl.ds(o, chunk), :], v_ref[1, pl.ds(o, chunk), :]])
        return carry

    lax.fori_loop(0, k_ref.shape[1] // chunk, body, 0, unroll=2)

    @pl.when(kj == nk - 1)
    def _():
        lane = lax.broadcasted_iota(jnp.int32, (1, LANES), 1)
        acc0 = acc_sc[0]
        acc1 = acc_sc[1]
        o0 = acc0 * (1.0 / acc0[:, V_DIM:V_DIM + 1])
        o1 = acc1 * (1.0 / acc1[:, 0:1])
        o_ref[...] = jnp.where(lane < V_DIM, o0, o1).astype(BF16)


def _attention(q, k_ctx, v_ctx, k_lat, v_lat, tq, tk, chunk):
    b_, nh, s_, hp = q.shape
    c_ = k_ctx.shape[2]
    return pl.pallas_call(
        functools.partial(_attn_kernel, chunk=chunk),
        grid=(b_, nh // 2, s_ // tq, s_ // tk),
        in_specs=[
            pl.BlockSpec((None, 2, tq, hp), lambda b, h, i, j: (b, h, i, 0)),
            pl.BlockSpec((None, 2, c_, hp), lambda b, h, i, j: (b, h, 0, 0)),
            pl.BlockSpec((None, 2, c_, hp), lambda b, h, i, j: (b, h, 0, 0)),
            pl.BlockSpec((None, 2, tk, hp), lambda b, h, i, j: (b, h, j, 0)),
            pl.BlockSpec((None, 2, tk, hp), lambda b, h, i, j: (b, h, j, 0)),
        ],
        out_specs=pl.BlockSpec((None, tq, hp), lambda b, h, i, j: (b, i, h)),
        out_shape=jax.ShapeDtypeStruct((b_, s_, (nh // 2) * hp), BF16),
        scratch_shapes=[pltpu.VMEM((2, tq, LANES), F32), pltpu.VMEM((2, tq, LANES), F32)],
        compiler_params=_cparams(("parallel", "parallel", "parallel", "arbitrary")),
        name="mla_attention",
    )(q, k_ctx, v_ctx, k_lat, v_lat)


def _gate_blocks(gate_w_d):
    hd = LRU_HEAD_DIM
    per = LRU_GROUP // hd
    ng = LRU_HEADS // per
    w = gate_w_d.reshape(2, ng, per, hd, hd)
    eye = jnp.eye(per, dtype=gate_w_d.dtype)
    blk = jnp.einsum('gnpij,pq->gnpiqj', w, eye).reshape(2, ng, LRU_GROUP, LRU_GROUP)
    return (0.5 * jnp.concatenate([blk[0], blk[1]], axis=-1)).astype(BF16)


def _pad_heads(w, per_head, offset=0):
    k = w.shape[0]
    w3 = w.reshape(k, MLA_HEADS, per_head)
    out = jnp.zeros((k, MLA_HEADS, HEAD_PAD), w.dtype)
    out = out.at[:, :, offset:offset + per_head].set(w3)
    return out.reshape(k, MLA_HEADS * HEAD_PAD)


def _rope_swap_idx():
    return np.arange(QK_ROPE) ^ ROPE_PAIRS


def _prep_mla(w_in, w_uq, w_ukv):
    d = w_in.shape[0]
    qc = w_in[:, :Q_LORA]
    kvc = w_in[:, Q_LORA:Q_LORA + KV_LORA]
    kr = w_in[:, Q_LORA + KV_LORA:Q_LORA + KV_LORA + QK_ROPE]
    g = w_in[:, Q_LORA + KV_LORA + QK_ROPE:]
    swap = _rope_swap_idx()
    kr_blk = jnp.zeros((d, HEAD_PAD), w_in.dtype).at[:, QK_NOPE:QK_NOPE + QK_ROPE].set(kr)
    krs_blk = jnp.zeros((d, HEAD_PAD), w_in.dtype).at[:, QK_NOPE:QK_NOPE + QK_ROPE].set(kr[:, swap])
    w_in_p = jnp.concatenate([qc, kvc, kr_blk, krs_blk, g], axis=1).astype(BF16)

    wq3 = w_uq.reshape(Q_LORA, MLA_HEADS, QK_DIM)
    wq_p = _pad_heads(w_uq, QK_DIM).astype(BF16)
    rope_sw = wq3[:, :, QK_NOPE:][:, :, swap]
    wqs = jnp.zeros((Q_LORA, MLA_HEADS, HEAD_PAD), w_uq.dtype).at[:, :, QK_NOPE:QK_NOPE + QK_ROPE].set(rope_sw)
    wqs_p = wqs.reshape(Q_LORA, MLA_HEADS * HEAD_PAD).astype(BF16)

    wkv3 = w_ukv.reshape(KV_LORA, MLA_HEADS, QK_NOPE + V_DIM)
    wk_p = _pad_heads(wkv3[:, :, :QK_NOPE].reshape(KV_LORA, -1), QK_NOPE).astype(BF16)
    wv3 = wkv3[:, :, QK_NOPE:]
    wv = jnp.zeros((KV_LORA, MLA_HEADS // 2, 2, HEAD_PAD), w_ukv.dtype)
    wv3p = wv3.reshape(KV_LORA, MLA_HEADS // 2, 2, V_DIM)
    wv = wv.at[:, :, 0, :V_DIM].set(wv3p[:, :, 0]).at[:, :, 1, V_DIM:].set(wv3p[:, :, 1])
    wv_p = wv.reshape(KV_LORA, MLA_HEADS * HEAD_PAD).astype(BF16)
    return w_in_p, wq_p, wqs_p, wk_p, wv_p


def _rope_tables(s_):
    rows = s_ // GRID_W
    inv = ROPE_THETA ** (-jnp.arange(ROPE_PAIRS, dtype=F32) / ROPE_PAIRS)
    a_r = jnp.arange(rows, dtype=F32)[:, None] * inv
    a_c = jnp.arange(GRID_W, dtype=F32)[:, None] * inv
    shape = (rows, GRID_W, ROPE_PAIRS)
    by_row = lambda t: jnp.broadcast_to(t[:, None, :], shape)
    by_col = lambda t: jnp.broadcast_to(t[None, :, :], shape)
    cr, sr, cc, sc = by_row(jnp.cos(a_r)), by_row(jnp.sin(a_r)), by_col(jnp.cos(a_c)), by_col(jnp.sin(a_c))
    pad = HEAD_PAD - QK_NOPE - QK_ROPE
    cos_t = jnp.concatenate([jnp.ones(shape[:2] + (QK_NOPE,), F32), cr, cr, cc, cc,
                             jnp.ones(shape[:2] + (pad,), F32)], axis=-1)
    sin_t = jnp.concatenate([jnp.zeros(shape[:2] + (QK_NOPE,), F32), -sr, sr, -sc, sc,
                             jnp.zeros(shape[:2] + (pad,), F32)], axis=-1)
    return cos_t.reshape(s_, HEAD_PAD), sin_t.reshape(s_, HEAD_PAD)


def kernel(x, c, ctx, c_ctx, ada_w, ada_b, ln_g, ln_b, w_in_rf, conv_w, conv_b, lru_gate_w, lru_gate_b,
           lru_lambda, fnet_w, fnet_b, w_out_rf, w_in_mla, q_norm_g, kv_norm_g, w_uq, w_ukv, w_out_mla):
    b_, s_, d = x.shape
    c_len = ctx.shape[1]
    tm = 512 if s_ % 512 == 0 else s_
    t_scan = 512 if s_ % 512 == 0 else s_

    cond = jnp.zeros((SUBLANES, d), F32).at[:b_].set(c).at[b_].set(c_ctx)
    mod = _adaln(cond, ada_w, ada_b).reshape(DEPTH, SUBLANES, 3, d)
    mod_lat = [mod[l, :b_] for l in range(DEPTH)]
    mod_ctx = [jnp.broadcast_to(mod[l, b_][None], (b_, 3, d)) for l in range(DEPTH)]

    w_in0 = w_in_rf[0].astype(BF16)
    w_out0 = w_out_rf[0].astype(BF16)
    fw_bf = fnet_w[0].astype(BF16)
    wg = [_gate_blocks(lru_gate_w[0, dd]) for dd in range(2)]
    cb = conv_b[0].reshape(1, -1)

    zl_c, zf_c, sg_c = _in0(ctx, mod_ctx[0], w_in0, c_len)
    zl_l, zf_l, sg_l = _in0(x, mod_lat[0], w_in0, tm)

    h_c, h_l = [], []
    for dd in range(2):
        rev = dd == 1
        lam = lru_lambda[0, dd].reshape(1, -1)
        zero = jnp.zeros((b_, 1, d), F32)
        hc, hc_fin = _lru_scan(zl_c, conv_w[0], cb, wg[dd], lru_gate_b[0, dd], lam, zero,
                               reverse=rev, t_rows=c_len)
        hl, _ = _lru_scan(zl_l, conv_w[0], cb, wg[dd], lru_gate_b[0, dd], lam, hc_fin,
                          reverse=rev, t_rows=t_scan)
        h_c.append(hc)
        h_l.append(hl)

    f_c = _fnet_ctx(zf_c, fw_bf, fnet_b[0])
    f_l = _fnet_lat(zf_l, fw_bf, fnet_b[0])

    lg0, lb0 = ln_g[0].reshape(1, d), ln_b[0].reshape(1, d)
    h1_c = _out0(h_c[0], h_c[1], f_c, sg_c, ctx, mod_ctx[0], w_out0, lg0, lb0, c_len)
    h1_l = _out0(h_l[0], h_l[1], f_l, sg_l, x, mod_lat[0], w_out0, lg0, lb0, tm)

    w_in_p, wq_p, wqs_p, wk_p, wv_p = _prep_mla(w_in_mla[0], w_uq[0], w_ukv[0])
    qg = q_norm_g[0].reshape(1, -1)
    kg = kv_norm_g[0].reshape(1, -1)
    cos_t, sin_t = _rope_tables(s_)

    k_c, v_c = _in1(h1_c, mod_ctx[1], w_in_p, qg, kg, wq_p, wqs_p, wk_p, wv_p, cos_t, sin_t, c_len,
                    want_q=False)
    q_l, k_l, v_l, sg1 = _in1(h1_l, mod_lat[1], w_in_p, qg, kg, wq_p, wqs_p, wk_p, wv_p, cos_t, sin_t, tm,
                              want_q=True)

    tq = 1024 if s_ % 1024 == 0 else s_
    tk = 8192 if s_ % 8192 == 0 else s_
    chunk = 1024 if tk % 1024 == 0 else tk
    o = _attention(q_l, k_c, v_c, k_l, v_l, tq, tk, chunk)

    lg1, lb1 = ln_g[1].reshape(1, d), ln_b[1].reshape(1, d)
    return _out1(o, sg1, h1_l, mod_lat[1], w_out_mla[0].astype(BF16), lg1, lb1, tm)
```

```python
import functools
import math

import numpy as np
import jax
import jax.numpy as jnp
from jax import lax
from jax.experimental import pallas as pl
from jax.experimental.pallas import tpu as pltpu

F32 = jnp.float32
BF16 = jnp.bfloat16

DEPTH = 2
GRID_W = 64
DEEPNORM_ALPHA = (2 * DEPTH) ** 0.25
LN_EPS = 1e-6
RMS_EPS = 1e-6
LRU_HEADS = 16
LRU_HEAD_DIM = 64
LRU_GROUP = 256
CONV_W = 4
LRU_C = 8.0
FNET_GROUPS = 4
FNET_GROUP_DIM = 128
MLA_HEADS = 16
Q_LORA = 256
KV_LORA = 128
QK_NOPE = 64
QK_ROPE = 32
V_DIM = 64
QK_DIM = QK_NOPE + QK_ROPE
ROPE_PAIRS = QK_ROPE // 4
ROPE_THETA = 10000.0
ATTN_SCALE = QK_DIM ** -0.5
LOG2E = 1.4426950408889634

LANES = 128
SUBLANES = 8
BF16_ROWS = 16
VMEM_LIMIT = 56 * 1024 * 1024

HEAD_PAD = LANES
FFT_N2 = 128
FFT_PITCH = FFT_N2 + SUBLANES
FFT_BATCH = 4
NEG_BIG = -1e30


def _sigmoid(x):
    return 0.5 * (jnp.tanh(0.5 * x) + 1.0)


def _silu(x):
    return x * _sigmoid(x)


def _cparams(sem):
    return pltpu.CompilerParams(dimension_semantics=sem, vmem_limit_bytes=VMEM_LIMIT)


def _adaln_kernel(cond_ref, w_ref, b_ref, o_ref):
    c = cond_ref[...]
    o_ref[...] = jnp.dot(_silu(c), w_ref[...], preferred_element_type=F32,
                         precision=lax.Precision.HIGHEST) + b_ref[...]


def _adaln(cond, ada_w, ada_b):
    depth, d, n = ada_w.shape
    rows = cond.shape[0]
    tn = 768
    return pl.pallas_call(
        _adaln_kernel,
        grid=(depth, n // tn),
        in_specs=[
            pl.BlockSpec((rows, d), lambda l, j: (0, 0)),
            pl.BlockSpec((None, d, tn), lambda l, j: (l, 0, j)),
            pl.BlockSpec((None, 1, tn), lambda l, j: (l, 0, j)),
        ],
        out_specs=pl.BlockSpec((None, rows, tn), lambda l, j: (l, 0, j)),
        out_shape=jax.ShapeDtypeStruct((depth, rows, n), F32),
        compiler_params=_cparams(("parallel", "parallel")),
        name="adaln",
    )(cond, ada_w, ada_b.reshape(depth, 1, n))


def _in0_kernel(x_ref, mod_ref, w_ref, zl_ref, zf_ref, sg_ref, *, lru_w, fn_w):
    shift = mod_ref[0:1, :]
    scale = mod_ref[1:2, :]
    u = (x_ref[...] * (1.0 + scale) + shift).astype(BF16)
    zl_ref[...] = jnp.dot(u, w_ref[:, :lru_w], preferred_element_type=F32).astype(BF16)
    zf_ref[...] = jnp.dot(u, w_ref[:, lru_w:lru_w + fn_w], preferred_element_type=F32).astype(BF16)
    g0 = lru_w + fn_w
    for c0 in range(0, lru_w + fn_w, 512):
        g = jnp.dot(u, w_ref[:, g0 + c0:g0 + c0 + 512], preferred_element_type=F32)
        sg_ref[:, c0:c0 + 512] = _silu(g).astype(BF16)


def _in0(x, mod, w_bf, tm):
    b_, l_, d = x.shape
    lru_w, fn_w = d, d // 2
    mix = lru_w + fn_w
    return pl.pallas_call(
        functools.partial(_in0_kernel, lru_w=lru_w, fn_w=fn_w),
        grid=(b_, l_ // tm),
        in_specs=[
            pl.BlockSpec((None, tm, d), lambda b, i: (b, i, 0)),
            pl.BlockSpec((None, 3, d), lambda b, i: (b, 0, 0)),
            pl.BlockSpec((d, 2 * mix), lambda b, i: (0, 0)),
        ],
        out_specs=[
            pl.BlockSpec((None, tm, lru_w), lambda b, i: (b, i, 0)),
            pl.BlockSpec((None, tm, fn_w), lambda b, i: (b, i, 0)),
            pl.BlockSpec((None, tm, mix), lambda b, i: (b, i, 0)),
        ],
        out_shape=[
            jax.ShapeDtypeStruct((b_, l_, lru_w), BF16),
            jax.ShapeDtypeStruct((b_, l_, fn_w), BF16),
            jax.ShapeDtypeStruct((b_, l_, mix), BF16),
        ],
        compiler_params=_cparams(("parallel", "parallel")),
        name="in_proj0",
    )(x, mod, w_bf)


def _scan_kernel(z_ref, zp_ref, zn_ref, cw_ref, cb_ref, wg_ref, gb_ref, lam_ref, h0_ref,
                 h_ref, hfin_ref, zz_sc, a_sc, b_sc, hs_sc, carry_sc, *, reverse, t_rows):
    i = pl.program_id(1)
    nt = pl.num_programs(1)
    tt = (nt - 1 - i) if reverse else i
    halo = BF16_ROWS

    @pl.when(i == 0)
    def _():
        carry_sc[...] = jnp.broadcast_to(h0_ref[...], carry_sc.shape)

    zz_sc[0:halo, :] = jnp.where(tt > 0, zp_ref[...].astype(F32), 0.0)
    zz_sc[halo:halo + t_rows, :] = z_ref[...].astype(F32)
    zz_sc[halo + t_rows:halo + t_rows + halo, :] = jnp.where(tt < nt - 1, zn_ref[...].astype(F32), 0.0)

    width = z_ref.shape[-1]
    nl = -lam_ref[...]
    half_c = (0.5 * LRU_C) * (jnp.maximum(nl, 0.0) + jnp.log(1.0 + jnp.exp(-jnp.abs(nl))))
    half_gb = 0.5 * gb_ref[...]
    n_buf = t_rows + 2 * halo
    for j in range(width // LRU_GROUP):
        cs = slice(j * LRU_GROUP, (j + 1) * LRU_GROUP)
        zf = zz_sc[:, cs]
        xc = cb_ref[:, cs] + cw_ref[1:2, cs] * zf[halo:halo + t_rows]
        for k, shift in ((0, 1), (2, n_buf - 1), (3, n_buf - 2)):
            xc = xc + cw_ref[k:k + 1, cs] * pltpu.roll(zf, shift, axis=0)[halo:halo + t_rows]
        g = jnp.dot(xc.astype(BF16), wg_ref[j], preferred_element_type=F32)
        t_r = jnp.tanh(g[:, :LRU_GROUP] + half_gb[0:1, cs])
        t_i = jnp.tanh(g[:, LRU_GROUP:] + half_gb[1:2, cs])
        hc = half_c[:, cs]
        nla = hc * t_r + hc
        a = jnp.exp2(nla * (-LOG2E))
        y = jnp.tanh(nla) * (1.0 + a * a)
        sq = jnp.where(y > 0.0, y * lax.rsqrt(y), 0.0)
        hx = 0.5 * xc
        a_sc[:, cs] = a
        b_sc[:, cs] = sq * (hx * t_i + hx)

    def group(gi, h):
        base = (t_rows - SUBLANES - gi * SUBLANES) if reverse else gi * SUBLANES
        base = pl.multiple_of(base, SUBLANES)
        for t in range(SUBLANES):
            row = base + ((SUBLANES - 1 - t) if reverse else t)
            h = a_sc[pl.ds(row, 1), :] * h + b_sc[pl.ds(row, 1), :]
            hs_sc[pl.ds(row, 1), :] = h
        return h

    h_last = lax.fori_loop(0, t_rows // SUBLANES, group, carry_sc[0:1, :])
    carry_sc[0:1, :] = h_last
    h_ref[...] = hs_sc[...].astype(BF16)

    @pl.when(i == nt - 1)
    def _():
        hfin_ref[...] = h_last


def _lru_scan(zl, conv_w, conv_b, wg_d, gb_d, lam_d, h0_d, *, reverse, t_rows):
    b_, l_, w = zl.shape
    nt = l_ // t_rows
    hb = t_rows // BF16_ROWS
    n_hb = l_ // BF16_ROWS

    def tile(i):
        return (nt - 1 - i) if reverse else i

    return pl.pallas_call(
        functools.partial(_scan_kernel, reverse=reverse, t_rows=t_rows),
        grid=(b_, nt),
        in_specs=[
            pl.BlockSpec((None, t_rows, w), lambda b, i: (b, tile(i), 0)),
            pl.BlockSpec((None, BF16_ROWS, w), lambda b, i: (b, jnp.maximum(tile(i) * hb - 1, 0), 0)),
            pl.BlockSpec((None, BF16_ROWS, w), lambda b, i: (b, jnp.minimum((tile(i) + 1) * hb, n_hb - 1), 0)),
            pl.BlockSpec((CONV_W, w), lambda b, i: (0, 0)),
            pl.BlockSpec((1, w), lambda b, i: (0, 0)),
            pl.BlockSpec((w // LRU_GROUP, LRU_GROUP, 2 * LRU_GROUP), lambda b, i: (0, 0, 0)),
            pl.BlockSpec((2, w), lambda b, i: (0, 0)),
            pl.BlockSpec((1, w), lambda b, i: (0, 0)),
            pl.BlockSpec((None, 1, w), lambda b, i: (b, 0, 0)),
        ],
        out_specs=[
            pl.BlockSpec((None, t_rows, w), lambda b, i: (b, tile(i), 0)),
            pl.BlockSpec((None, 1, w), lambda b, i: (b, 0, 0)),
        ],
        out_shape=[
            jax.ShapeDtypeStruct((b_, l_, w), BF16),
            jax.ShapeDtypeStruct((b_, 1, w), F32),
        ],
        scratch_shapes=[
            pltpu.VMEM((t_rows + 2 * BF16_ROWS, w), F32),
            pltpu.VMEM((t_rows, w), F32),
            pltpu.VMEM((t_rows, w), F32),
            pltpu.VMEM((t_rows, w), F32),
            pltpu.VMEM((SUBLANES, w), F32),
        ],
        compiler_params=_cparams(("parallel", "arbitrary")),
        name="lru_scan_bwd" if reverse else "lru_scan_fwd",
    )(zl, zl, zl, conv_w, conv_b, wg_d, gb_d, lam_d, h0_d)


def _dft_cos_sin(n):
    k = np.arange(n, dtype=np.int64)
    ang = 2.0 * np.pi * ((k[:, None] * k[None, :]) % n).astype(np.float64) / n
    return np.cos(ang), np.sin(ang)


def _mxu_const(a):
    return jnp.asarray(a, F32).astype(BF16)


def _fnet_lat_kernel(x_ref, fc_ref, m1_ref, c2s2_ref, twa_ref, twb_ref, fw_ref, fb_ref, o_ref, s_sc,
                     *, n1, norm):
    n2 = FFT_N2
    pitch = FFT_PITCH
    gd = x_ref.shape[-1]

    nb = FFT_BATCH

    def stage0(i, carry):
        r0 = pl.multiple_of(i * (nb * n2), nb * n2)
        u = jnp.dot(x_ref[pl.ds(r0, nb * n2), :], fc_ref[...], preferred_element_type=F32)
        for j in range(nb):
            d0 = pl.multiple_of((i * nb + j) * pitch, SUBLANES)
            s_sc[0, pl.ds(d0, n2), :] = u[j * n2:(j + 1) * n2, :gd]
            s_sc[1, pl.ds(d0, n2), :] = u[j * n2:(j + 1) * n2, gd:]
        return carry

    lax.fori_loop(0, n1 // nb, stage0, 0)

    def stage1(a, carry):
        ta_r = twa_ref[0, a]
        ta_i = twa_ref[1, a]
        gs = []
        for b in range(SUBLANES):
            col = a * SUBLANES + b
            g_r = s_sc[0, pl.ds(col, n1, stride=pitch), :]
            g_i = s_sc[1, pl.ds(col, n1, stride=pitch), :]
            gs.append(jnp.concatenate([g_r, g_i], axis=0).astype(BF16))
        y_all = jnp.dot(m1_ref[...], jnp.concatenate(gs, axis=1), preferred_element_type=F32)
        for b in range(SUBLANES):
            col = a * SUBLANES + b
            y_r = y_all[:n1, b * gd:(b + 1) * gd]
            y_i = y_all[n1:, b * gd:(b + 1) * gd]
            tb_r = twb_ref[0, b]
            tb_i = twb_ref[1, b]
            c = ta_r * tb_r - ta_i * tb_i
            s = -(ta_r * tb_i + ta_i * tb_r)
            s_sc[0, pl.ds(col, n1, stride=pitch), :] = y_r * c + y_i * s
            s_sc[1, pl.ds(col, n1, stride=pitch), :] = y_i * c - y_r * s
        return carry

    lax.fori_loop(0, n2 // SUBLANES, stage1, 0)

    def stage2(i, carry):
        zs = []
        for j in range(nb):
            d0 = pl.multiple_of((i * nb + j) * pitch, SUBLANES)
            zs.append(jnp.concatenate([s_sc[0, pl.ds(d0, n2), :], s_sc[1, pl.ds(d0, n2), :]],
                                      axis=0).astype(BF16))
        z = jnp.concatenate(zs, axis=1)
        xr = jnp.dot(c2s2_ref[...], z, preferred_element_type=F32) * norm
        xs = jnp.concatenate([xr[:, j * gd:(j + 1) * gd] for j in range(nb)], axis=0).astype(BF16)
        out = jnp.dot(xs, fw_ref[...], preferred_element_type=F32) + fb_ref[...]
        for j in range(nb):
            o_ref[pl.ds(i * nb + j, n2, stride=n1), :] = out[j * n2:(j + 1) * n2, :]
        return carry

    lax.fori_loop(0, n1 // nb, stage2, 0)


def _fnet_lat(zf, fnet_w_bf, fnet_b):
    b_, l_, fw = zf.shape
    gd = FNET_GROUP_DIM
    groups = fw // gd
    n2 = FFT_N2
    n1 = l_ // n2
    assert n1 * n2 == l_ and n1 % SUBLANES == 0
    cc, sc = _dft_cos_sin(gd)
    fc = _mxu_const(np.concatenate([cc, -sc], axis=1))
    c1, s1 = _dft_cos_sin(n1)
    m1 = _mxu_const(np.block([[c1, s1], [-s1, c1]]))
    c2, s2 = _dft_cos_sin(n2)
    c2s2 = _mxu_const(np.concatenate([c2, s2], axis=1))
    k1 = np.arange(n1, dtype=np.float64)[None, :, None]
    ang_a = 2.0 * np.pi * (np.arange(n2 // SUBLANES, dtype=np.float64) * SUBLANES)[:, None, None] * k1 / l_
    ang_b = 2.0 * np.pi * np.arange(SUBLANES, dtype=np.float64)[:, None, None] * k1 / l_
    ones = np.ones((1, 1, gd))
    twa = jnp.asarray(np.stack([np.cos(ang_a) * ones, -np.sin(ang_a) * ones]), F32)
    twb = jnp.asarray(np.stack([np.cos(ang_b) * ones, -np.sin(ang_b) * ones]), F32)
    norm = 1.0 / math.sqrt(l_ * gd)
    const2 = lambda b, g: (0, 0)
    const4 = lambda b, g: (0, 0, 0, 0)
    return pl.pallas_call(
        functools.partial(_fnet_lat_kernel, n1=n1, norm=norm),
        grid=(b_, groups),
        in_specs=[
            pl.BlockSpec((None, l_, gd), lambda b, g: (b, 0, g)),
            pl.BlockSpec(fc.shape, const2),
            pl.BlockSpec(m1.shape, const2),
            pl.BlockSpec(c2s2.shape, const2),
            pl.BlockSpec(twa.shape, const4),
            pl.BlockSpec(twb.shape, const4),
            pl.BlockSpec((None, gd, gd), lambda b, g: (g, 0, 0)),
            pl.BlockSpec((None, 1, gd), lambda b, g: (g, 0, 0)),
        ],
        out_specs=pl.BlockSpec((None, l_, gd), lambda b, g: (b, 0, g)),
        out_shape=jax.ShapeDtypeStruct((b_, l_, fw), F32),
        scratch_shapes=[pltpu.VMEM((2, n1 * FFT_PITCH, gd), F32)],
        compiler_params=_cparams(("parallel", "parallel")),
        name="fnet_lat",
    )(zf, fc, m1, c2s2, twa, twb, fnet_w_bf, fnet_b.reshape(groups, 1, gd))


def _fnet_ctx_kernel(x_ref, fc_ref, cs_ref, fw_ref, fb_ref, o_ref, *, norm):
    gd = x_ref.shape[-1]
    u = jnp.dot(x_ref[...], fc_ref[...], preferred_element_type=F32)
    z = jnp.concatenate([u[:, :gd], u[:, gd:]], axis=0).astype(BF16)
    xr = jnp.dot(cs_ref[...], z, preferred_element_type=F32) * norm
    o_ref[...] = jnp.dot(xr.astype(BF16), fw_ref[...], preferred_element_type=F32) + fb_ref[...]


def _fnet_ctx(zf, fnet_w_bf, fnet_b):
    b_, l_, fw = zf.shape
    gd = FNET_GROUP_DIM
    groups = fw // gd
    cc, sc = _dft_cos_sin(gd)
    fc = _mxu_const(np.concatenate([cc, -sc], axis=1))
    cl, sl = _dft_cos_sin(l_)
    cs = _mxu_const(np.concatenate([cl, sl], axis=1))
    norm = 1.0 / math.sqrt(l_ * gd)
    const2 = lambda b, g: (0, 0)
    return pl.pallas_call(
        functools.partial(_fnet_ctx_kernel, norm=norm),
        grid=(b_, groups),
        in_specs=[
            pl.BlockSpec((None, l_, gd), lambda b, g: (b, 0, g)),
            pl.BlockSpec(fc.shape, const2),
            pl.BlockSpec(cs.shape, const2),
            pl.BlockSpec((None, gd, gd), lambda b, g: (g, 0, 0)),
            pl.BlockSpec((None, 1, gd), lambda b, g: (g, 0, 0)),
        ],
        out_specs=pl.BlockSpec((None, l_, gd), lambda b, g: (b, 0, g)),
        out_shape=jax.ShapeDtypeStruct((b_, l_, fw), F32),
        compiler_params=_cparams(("parallel", "parallel")),
        name="fnet_ctx",
    )(zf, fc, cs, fnet_w_bf, fnet_b.reshape(groups, 1, gd))


def _deepnorm(resid, gate, y, g, b):
    v = DEEPNORM_ALPHA * resid + gate * y
    mu = jnp.mean(v, axis=-1, keepdims=True)
    vc = v - mu
    var = jnp.mean(vc * vc, axis=-1, keepdims=True)
    return vc * lax.rsqrt(var + LN_EPS) * g + b


def _out0_kernel(hf_ref, hb_ref, f_ref, sg_ref, x_ref, mod_ref, w_ref, lg_ref, lb_ref, o_ref, *, lru_w):
    r = hf_ref[...].astype(F32) + hb_ref[...].astype(F32)
    m_l = (r * sg_ref[:, :lru_w].astype(F32)).astype(BF16)
    m_f = (f_ref[...] * sg_ref[:, lru_w:].astype(F32)).astype(BF16)
    y = jnp.dot(m_l, w_ref[:lru_w, :], preferred_element_type=F32)
    y = y + jnp.dot(m_f, w_ref[lru_w:, :], preferred_element_type=F32)
    o_ref[...] = _deepnorm(x_ref[...], mod_ref[2:3, :], y, lg_ref[...], lb_ref[...])


def _out0(hf, hb, f, sg, x, mod, w_bf, ln_g, ln_b, tm):
    b_, l_, d = x.shape
    lru_w = hf.shape[-1]
    fn_w = f.shape[-1]
    mix = lru_w + fn_w
    row = lambda b, i: (b, i, 0)
    const2 = lambda b, i: (0, 0)
    return pl.pallas_call(
        functools.partial(_out0_kernel, lru_w=lru_w),
        grid=(b_, l_ // tm),
        in_specs=[
            pl.BlockSpec((None, tm, lru_w), row),
            pl.BlockSpec((None, tm, lru_w), row),
            pl.BlockSpec((None, tm, fn_w), row),
            pl.BlockSpec((None, tm, mix), row),
            pl.BlockSpec((None, tm, d), row),
            pl.BlockSpec((None, 3, d), lambda b, i: (b, 0, 0)),
            pl.BlockSpec((mix, d), const2),
            pl.BlockSpec((1, d), const2),
            pl.BlockSpec((1, d), const2),
        ],
        out_specs=pl.BlockSpec((None, tm, d), row),
        out_shape=jax.ShapeDtypeStruct((b_, l_, d), F32),
        compiler_params=_cparams(("parallel", "parallel")),
        name="out_proj0",
    )(hf, hb, f, sg, x, mod, w_bf, ln_g, ln_b)


def _out1_kernel(o_ref_in, sg_ref, x_ref, mod_ref, w_ref, lg_ref, lb_ref, o_ref):
    m = (o_ref_in[...].astype(F32) * sg_ref[...].astype(F32)).astype(BF16)
    y = jnp.dot(m, w_ref[...], preferred_element_type=F32)
    o_ref[...] = _deepnorm(x_ref[...], mod_ref[2:3, :], y, lg_ref[...], lb_ref[...])


def _out1(o, sg, x, mod, w_bf, ln_g, ln_b, tm):
    b_, l_, d = x.shape
    wdt = o.shape[-1]
    row = lambda b, i: (b, i, 0)
    const2 = lambda b, i: (0, 0)
    return pl.pallas_call(
        _out1_kernel,
        grid=(b_, l_ // tm),
        in_specs=[
            pl.BlockSpec((None, tm, wdt), row),
            pl.BlockSpec((None, tm, wdt), row),
            pl.BlockSpec((None, tm, d), row),
            pl.BlockSpec((None, 3, d), lambda b, i: (b, 0, 0)),
            pl.BlockSpec((wdt, d), const2),
            pl.BlockSpec((1, d), const2),
            pl.BlockSpec((1, d), const2),
        ],
        out_specs=pl.BlockSpec((None, tm, d), row),
        out_shape=jax.ShapeDtypeStruct((b_, l_, d), F32),
        compiler_params=_cparams(("parallel", "parallel")),
        name="out_proj1",
    )(o, sg, x, mod, w_bf, ln_g, ln_b)


def _rms(x, g):
    return x * lax.rsqrt(jnp.mean(x * x, axis=-1, keepdims=True) + RMS_EPS) * g


def _in1_kernel(*refs, rope, want_q):
    if want_q:
        (x_ref, mod_ref, w_ref, qg_ref, kg_ref, wq_ref, wqs_ref, wk_ref, wv_ref, cos_ref, sin_ref,
         q_ref, k_ref, v_ref, sg_ref) = refs
    else:
        (x_ref, mod_ref, w_ref, kg_ref, wk_ref, wv_ref, k_ref, v_ref) = refs
    shift = mod_ref[0:1, :]
    scale = mod_ref[1:2, :]
    u = (x_ref[...] * (1.0 + scale) + shift).astype(BF16)
    hp = HEAD_PAD
    o_kv = Q_LORA
    o_kr = Q_LORA + KV_LORA
    o_g = o_kr + 2 * hp

    kvn = _rms(jnp.dot(u, w_ref[:, o_kv:o_kv + KV_LORA], preferred_element_type=F32), kg_ref[...]).astype(BF16)
    kr = jnp.dot(u, w_ref[:, o_kr:o_kr + 2 * hp], preferred_element_type=F32)
    if rope:
        cos = cos_ref[...]
        sin = sin_ref[...]
        krope = kr[:, :hp] * cos + kr[:, hp:] * sin
    else:
        krope = kr[:, :hp]
    lane = lax.broadcasted_iota(jnp.int32, (1, hp), 1)
    one_even = (lane == V_DIM).astype(F32)
    one_odd = (lane == 0).astype(F32)
    for h2 in range(MLA_HEADS // 2):
        cs = slice(2 * h2 * hp, (2 * h2 + 2) * hp)
        kn = jnp.dot(kvn, wk_ref[:, cs], preferred_element_type=F32)
        vv = jnp.dot(kvn, wv_ref[:, cs], preferred_element_type=F32)
        k_ref[2 * h2] = (kn[:, :hp] + krope).astype(BF16)
        k_ref[2 * h2 + 1] = (kn[:, hp:] + krope).astype(BF16)
        v_ref[2 * h2] = (vv[:, :hp] + one_even).astype(BF16)
        v_ref[2 * h2 + 1] = (vv[:, hp:] + one_odd).astype(BF16)

    if want_q:
        qn = _rms(jnp.dot(u, w_ref[:, :Q_LORA], preferred_element_type=F32), qg_ref[...]).astype(BF16)
        qscale = ATTN_SCALE * LOG2E
        cq = cos * qscale
        sq = sin * qscale
        for h2 in range(MLA_HEADS // 2):
            cs = slice(2 * h2 * hp, (2 * h2 + 2) * hp)
            z1 = jnp.dot(qn, wq_ref[:, cs], preferred_element_type=F32)
            z2 = jnp.dot(qn, wqs_ref[:, cs], preferred_element_type=F32)
            q_ref[2 * h2] = (z1[:, :hp] * cq + z2[:, :hp] * sq).astype(BF16)
            q_ref[2 * h2 + 1] = (z1[:, hp:] * cq + z2[:, hp:] * sq).astype(BF16)
        gw = sg_ref.shape[-1]
        for c0 in range(0, gw, 512):
            g = jnp.dot(u, w_ref[:, o_g + c0:o_g + c0 + 512], preferred_element_type=F32)
            sg_ref[:, c0:c0 + 512] = _silu(g).astype(BF16)


def _in1(x, mod, w_in_p, qg, kg, wq_p, wqs_p, wk_p, wv_p, cos_t, sin_t, tm, *, want_q):
    b_, l_, d = x.shape
    hp = HEAD_PAD
    nh = MLA_HEADS
    gw = nh * V_DIM
    row = lambda b, i: (b, i, 0)
    const2 = lambda b, i: (0, 0)
    head = lambda b, i: (b, 0, i, 0)
    kv_specs = [pl.BlockSpec((None, nh, tm, hp), head), pl.BlockSpec((None, nh, tm, hp), head)]
    kv_shapes = [jax.ShapeDtypeStruct((b_, nh, l_, hp), BF16)] * 2
    if want_q:
        ins = [x, mod, w_in_p, qg, kg, wq_p, wqs_p, wk_p, wv_p, cos_t, sin_t]
        in_specs = [
            pl.BlockSpec((None, tm, d), row),
            pl.BlockSpec((None, 3, d), lambda b, i: (b, 0, 0)),
            pl.BlockSpec(w_in_p.shape, const2),
            pl.BlockSpec(qg.shape, const2),
            pl.BlockSpec(kg.shape, const2),
            pl.BlockSpec(wq_p.shape, const2),
            pl.BlockSpec(wqs_p.shape, const2),
            pl.BlockSpec(wk_p.shape, const2),
            pl.BlockSpec(wv_p.shape, const2),
            pl.BlockSpec((tm, hp), lambda b, i: (i, 0)),
            pl.BlockSpec((tm, hp), lambda b, i: (i, 0)),
        ]
        out_specs = [pl.BlockSpec((None, nh, tm, hp), head)] + kv_specs + [pl.BlockSpec((None, tm, gw), row)]
        out_shape = [jax.ShapeDtypeStruct((b_, nh, l_, hp), BF16)] + kv_shapes + [
            jax.ShapeDtypeStruct((b_, l_, gw), BF16)]
    else:
        ins = [x, mod, w_in_p, kg, wk_p, wv_p]
        in_specs = [
            pl.BlockSpec((None, tm, d), row),
            pl.BlockSpec((None, 3, d), lambda b, i: (b, 0, 0)),
            pl.BlockSpec(w_in_p.shape, const2),
            pl.BlockSpec(kg.shape, const2),
            pl.BlockSpec(wk_p.shape, const2),
            pl.BlockSpec(wv_p.shape, const2),
        ]
        out_specs = kv_specs
        out_shape = kv_shapes
    return pl.pallas_call(
        functools.partial(_in1_kernel, rope=want_q, want_q=want_q),
        grid=(b_, l_ // tm),
        in_specs=in_specs,
        out_specs=out_specs,
        out_shape=out_shape,
        compiler_params=_cparams(("parallel", "parallel")),
        name="in_proj1" if want_q else "in_proj1_ctx",
    )(*ins)


def _attn_kernel(q_ref, kc_ref, vc_ref, k_ref, v_ref, o_ref, m_sc, acc_sc, *, chunk):
    kj = pl.program_id(3)
    nk = pl.num_programs(3)

    @pl.when(kj == 0)
    def _():
        m_sc[...] = jnp.full(m_sc.shape, NEG_BIG, F32)
        acc_sc[...] = jnp.zeros(acc_sc.shape, F32)

    def pair_update(ks, vs):
        ss = [lax.dot_general(q_ref[hh], ks[hh], (((1,), (1,)), ((), ())), preferred_element_type=F32)
              for hh in range(2)]
        for hh in range(2):
            s = ss[hh]
            m_prev = m_sc[hh]
            m_new = jnp.maximum(m_prev, jnp.max(s, axis=-1, keepdims=True))
            alpha = jnp.exp2(m_prev - m_new)
            p = jnp.concatenate(
                [jnp.exp2(s[:, c:c + LANES] - m_new).astype(BF16) for c in range(0, s.shape[1], LANES)],
                axis=1)
            acc_sc[hh] = alpha * acc_sc[hh] + jnp.dot(p, vs[hh], preferred_element_type=F32)
            m_sc[hh] = m_new

    @pl.when(kj == 0)
    def _():
        pair_update([kc_ref[0], kc_ref[1]], [vc_ref[0], vc_ref[1]])

    def body(ci, carry):
        o = pl.multiple_of(ci * chunk, chunk)
        pair_update([k_ref[0, pl.ds(o, chunk), :], k_ref[1, pl.ds(o, chunk), :]],
                    [v_ref[0, pl.ds(o, chunk), :], v_ref[1, pl.ds(o, chunk), :]])
        return carry

    lax.fori_loop(0, k_ref.shape[1] // chunk, body, 0, unroll=4)

    @pl.when(kj == nk - 1)
    def _():
        lane = lax.broadcasted_iota(jnp.int32, (1, LANES), 1)
        acc0 = acc_sc[0]
        acc1 = acc_sc[1]
        o0 = acc0 * (1.0 / acc0[:, V_DIM:V_DIM + 1])
        o1 = acc1 * (1.0 / acc1[:, 0:1])
        o_ref[...] = jnp.where(lane < V_DIM, o0, o1).astype(BF16)


def _attention(q, k_ctx, v_ctx, k_lat, v_lat, tq, tk, chunk):
    b_, nh, s_, hp = q.shape
    c_ = k_ctx.shape[2]
    return pl.pallas_call(
        functools.partial(_attn_kernel, chunk=chunk),
        grid=(b_, nh // 2, s_ // tq, s_ // tk),
        in_specs=[
            pl.BlockSpec((None, 2, tq, hp), lambda b, h, i, j: (b, h, i, 0)),
            pl.BlockSpec((None, 2, c_, hp), lambda b, h, i, j: (b, h, 0, 0)),
            pl.BlockSpec((None, 2, c_, hp), lambda b, h, i, j: (b, h, 0, 0)),
            pl.BlockSpec((None, 2, tk, hp), lambda b, h, i, j: (b, h, j, 0)),
            pl.BlockSpec((None, 2, tk, hp), lambda b, h, i, j: (b, h, j, 0)),
        ],
        out_specs=pl.BlockSpec((None, tq, hp), lambda b, h, i, j: (b, i, h)),
        out_shape=jax.ShapeDtypeStruct((b_, s_, (nh // 2) * hp), BF16),
        scratch_shapes=[pltpu.VMEM((2, tq, LANES), F32), pltpu.VMEM((2, tq, LANES), F32)],
        compiler_params=_cparams(("parallel", "parallel", "parallel", "arbitrary")),
        name="mla_attention",
    )(q, k_ctx, v_ctx, k_lat, v_lat)


def _gate_blocks(gate_w_d):
    hd = LRU_HEAD_DIM
    per = LRU_GROUP // hd
    ng = LRU_HEADS // per
    w = gate_w_d.reshape(2, ng, per, hd, hd)
    eye = jnp.eye(per, dtype=gate_w_d.dtype)
    blk = jnp.einsum('gnpij,pq->gnpiqj', w, eye).reshape(2, ng, LRU_GROUP, LRU_GROUP)
    return (0.5 * jnp.concatenate([blk[0], blk[1]], axis=-1)).astype(BF16)


def _pad_heads(w, per_head, offset=0):
    k = w.shape[0]
    w3 = w.reshape(k, MLA_HEADS, per_head)
    out = jnp.zeros((k, MLA_HEADS, HEAD_PAD), w.dtype)
    out = out.at[:, :, offset:offset + per_head].set(w3)
    return out.reshape(k, MLA_HEADS * HEAD_PAD)


def _rope_swap_idx():
    return np.arange(QK_ROPE) ^ ROPE_PAIRS


def _prep_mla(w_in, w_uq, w_ukv):
    d = w_in.shape[0]
    qc = w_in[:, :Q_LORA]
    kvc = w_in[:, Q_LORA:Q_LORA + KV_LORA]
    kr = w_in[:, Q_LORA + KV_LORA:Q_LORA + KV_LORA + QK_ROPE]
    g = w_in[:, Q_LORA + KV_LORA + QK_ROPE:]
    swap = _rope_swap_idx()
    kr_blk = jnp.zeros((d, HEAD_PAD), w_in.dtype).at[:, QK_NOPE:QK_NOPE + QK_ROPE].set(kr)
    krs_blk = jnp.zeros((d, HEAD_PAD), w_in.dtype).at[:, QK_NOPE:QK_NOPE + QK_ROPE].set(kr[:, swap])
    w_in_p = jnp.concatenate([qc, kvc, kr_blk, krs_blk, g], axis=1).astype(BF16)

    wq3 = w_uq.reshape(Q_LORA, MLA_HEADS, QK_DIM)
    wq_p = _pad_heads(w_uq, QK_DIM).astype(BF16)
    rope_sw = wq3[:, :, QK_NOPE:][:, :, swap]
    wqs = jnp.zeros((Q_LORA, MLA_HEADS, HEAD_PAD), w_uq.dtype).at[:, :, QK_NOPE:QK_NOPE + QK_ROPE].set(rope_sw)
    wqs_p = wqs.reshape(Q_LORA, MLA_HEADS * HEAD_PAD).astype(BF16)

    wkv3 = w_ukv.reshape(KV_LORA, MLA_HEADS, QK_NOPE + V_DIM)
    wk_p = _pad_heads(wkv3[:, :, :QK_NOPE].reshape(KV_LORA, -1), QK_NOPE).astype(BF16)
    wv3 = wkv3[:, :, QK_NOPE:].reshape(KV_LORA, MLA_HEADS // 2, 2 * V_DIM)
    zero = jnp.zeros_like(wv3[:, :, :V_DIM])
    wv_p = jnp.concatenate([wv3[:, :, :V_DIM], zero, zero, wv3[:, :, V_DIM:]], axis=-1)
    wv_p = wv_p.reshape(KV_LORA, MLA_HEADS * HEAD_PAD).astype(BF16)
    return w_in_p, wq_p, wqs_p, wk_p, wv_p


def _rope_tables(s_):
    rows = s_ // GRID_W
    inv = ROPE_THETA ** (-jnp.arange(ROPE_PAIRS, dtype=F32) / ROPE_PAIRS)
    a_r = jnp.arange(rows, dtype=F32)[:, None] * inv
    a_c = jnp.arange(GRID_W, dtype=F32)[:, None] * inv
    shape = (rows, GRID_W, ROPE_PAIRS)
    by_row = lambda t: jnp.broadcast_to(t[:, None, :], shape)
    by_col = lambda t: jnp.broadcast_to(t[None, :, :], shape)
    cr, sr, cc, sc = by_row(jnp.cos(a_r)), by_row(jnp.sin(a_r)), by_col(jnp.cos(a_c)), by_col(jnp.sin(a_c))
    pad = HEAD_PAD - QK_NOPE - QK_ROPE
    cos_t = jnp.concatenate([jnp.ones(shape[:2] + (QK_NOPE,), F32), cr, cr, cc, cc,
                             jnp.ones(shape[:2] + (pad,), F32)], axis=-1)
    sin_t = jnp.concatenate([jnp.zeros(shape[:2] + (QK_NOPE,), F32), -sr, sr, -sc, sc,
                             jnp.zeros(shape[:2] + (pad,), F32)], axis=-1)
    return cos_t.reshape(s_, HEAD_PAD), sin_t.reshape(s_, HEAD_PAD)


def kernel(x, c, ctx, c_ctx, ada_w, ada_b, ln_g, ln_b, w_in_rf, conv_w, conv_b, lru_gate_w, lru_gate_b,
           lru_lambda, fnet_w, fnet_b, w_out_rf, w_in_mla, q_norm_g, kv_norm_g, w_uq, w_ukv, w_out_mla):
    b_, s_, d = x.shape
    c_len = ctx.shape[1]
    tm = 512 if s_ % 512 == 0 else s_
    t_scan = 512 if s_ % 512 == 0 else s_

    cond = jnp.zeros((SUBLANES, d), F32).at[:b_].set(c).at[b_].set(c_ctx)
    mod = _adaln(cond, ada_w, ada_b).reshape(DEPTH, SUBLANES, 3, d)
    mod_lat = [mod[l, :b_] for l in range(DEPTH)]
    mod_ctx = [jnp.broadcast_to(mod[l, b_][None], (b_, 3, d)) for l in range(DEPTH)]

    w_in0 = w_in_rf[0].astype(BF16)
    w_out0 = w_out_rf[0].astype(BF16)
    fw_bf = fnet_w[0].astype(BF16)
    wg = [_gate_blocks(lru_gate_w[0, dd]) for dd in range(2)]
    cb = conv_b[0].reshape(1, -1)

    zl_c, zf_c, sg_c = _in0(ctx, mod_ctx[0], w_in0, c_len)
    zl_l, zf_l, sg_l = _in0(x, mod_lat[0], w_in0, tm)

    h_c, h_l = [], []
    for dd in range(2):
        rev = dd == 1
        lam = lru_lambda[0, dd].reshape(1, -1)
        zero = jnp.zeros((b_, 1, d), F32)
        hc, hc_fin = _lru_scan(zl_c, conv_w[0], cb, wg[dd], lru_gate_b[0, dd], lam, zero,
                               reverse=rev, t_rows=c_len)
        hl, _ = _lru_scan(zl_l, conv_w[0], cb, wg[dd], lru_gate_b[0, dd], lam, hc_fin,
                          reverse=rev, t_rows=t_scan)
        h_c.append(hc)
        h_l.append(hl)

    f_c = _fnet_ctx(zf_c, fw_bf, fnet_b[0])
    f_l = _fnet_lat(zf_l, fw_bf, fnet_b[0])

    lg0, lb0 = ln_g[0].reshape(1, d), ln_b[0].reshape(1, d)
    h1_c = _out0(h_c[0], h_c[1], f_c, sg_c, ctx, mod_ctx[0], w_out0, lg0, lb0, c_len)
    h1_l = _out0(h_l[0], h_l[1], f_l, sg_l, x, mod_lat[0], w_out0, lg0, lb0, tm)

    w_in_p, wq_p, wqs_p, wk_p, wv_p = _prep_mla(w_in_mla[0], w_uq[0], w_ukv[0])
    qg = q_norm_g[0].reshape(1, -1)
    kg = kv_norm_g[0].reshape(1, -1)
    cos_t, sin_t = _rope_tables(s_)

    k_c, v_c = _in1(h1_c, mod_ctx[1], w_in_p, qg, kg, wq_p, wqs_p, wk_p, wv_p, cos_t, sin_t, c_len,
                    want_q=False)
    q_l, k_l, v_l, sg1 = _in1(h1_l, mod_lat[1], w_in_p, qg, kg, wq_p, wqs_p, wk_p, wv_p, cos_t, sin_t, tm,
                              want_q=True)

    tq = 1024 if s_ % 1024 == 0 else s_
    tk = 8192 if s_ % 8192 == 0 else s_
    chunk = 1024 if tk % 1024 == 0 else tk
    o = _attention(q_l, k_c, v_c, k_l, v_l, tq, tk, chunk)

    lg1, lb1 = ln_g[1].reshape(1, d), ln_b[1].reshape(1, d)
    return _out1(o, sg1, h1_l, mod_lat[1], w_out_mla[0].astype(BF16), lg1, lb1, tm)
```

```python
import functools
import math

import numpy as np
import jax
import jax.numpy as jnp
from jax import lax
from jax.experimental import pallas as pl
from jax.experimental.pallas import tpu as pltpu

F32 = jnp.float32
BF16 = jnp.bfloat16

DEPTH = 2
GRID_W = 64
DEEPNORM_ALPHA = (2 * DEPTH) ** 0.25
LN_EPS = 1e-6
RMS_EPS = 1e-6
LRU_HEADS = 16
LRU_HEAD_DIM = 64
LRU_GROUP = 256
CONV_W = 4
LRU_C = 8.0
FNET_GROUPS = 4
FNET_GROUP_DIM = 128
MLA_HEADS = 16
Q_LORA = 256
KV_LORA = 128
QK_NOPE = 64
QK_ROPE = 32
V_DIM = 64
QK_DIM = QK_NOPE + QK_ROPE
ROPE_PAIRS = QK_ROPE // 4
ROPE_THETA = 10000.0
ATTN_SCALE = QK_DIM ** -0.5
LOG2E = 1.4426950408889634

LANES = 128
SUBLANES = 8
BF16_ROWS = 16
VMEM_LIMIT = 56 * 1024 * 1024

HEAD_PAD = LANES
VT_ROWS = V_DIM + BF16_ROWS
ATTN_SUB = 256
ATTN_UNROLL = 8
FFT_N2 = 128
FFT_PITCH = FFT_N2 + SUBLANES
FFT_BATCH = 4
NEG_BIG = -1e30


def _sigmoid(x):
    return 0.5 * (jnp.tanh(0.5 * x) + 1.0)


def _silu(x):
    return x * _sigmoid(x)


def _cparams(sem):
    return pltpu.CompilerParams(dimension_semantics=sem, vmem_limit_bytes=VMEM_LIMIT)


def _adaln_kernel(cond_ref, w_ref, b_ref, o_ref):
    c = cond_ref[...]
    o_ref[...] = jnp.dot(_silu(c), w_ref[...], preferred_element_type=F32,
                         precision=lax.Precision.HIGHEST) + b_ref[...]


def _adaln(cond, ada_w, ada_b):
    depth, d, n = ada_w.shape
    rows = cond.shape[0]
    tn = 768
    return pl.pallas_call(
        _adaln_kernel,
        grid=(depth, n // tn),
        in_specs=[
            pl.BlockSpec((rows, d), lambda l, j: (0, 0)),
            pl.BlockSpec((None, d, tn), lambda l, j: (l, 0, j)),
            pl.BlockSpec((None, 1, tn), lambda l, j: (l, 0, j)),
        ],
        out_specs=pl.BlockSpec((None, rows, tn), lambda l, j: (l, 0, j)),
        out_shape=jax.ShapeDtypeStruct((depth, rows, n), F32),
        compiler_params=_cparams(("parallel", "parallel")),
        name="adaln",
    )(cond, ada_w, ada_b.reshape(depth, 1, n))


def _in0_kernel(x_ref, mod_ref, w_ref, zl_ref, zf_ref, sg_ref, *, lru_w, fn_w):
    shift = mod_ref[0:1, :]
    scale = mod_ref[1:2, :]
    u = (x_ref[...] * (1.0 + scale) + shift).astype(BF16)
    zl_ref[...] = jnp.dot(u, w_ref[:, :lru_w], preferred_element_type=F32).astype(BF16)
    zf_ref[...] = jnp.dot(u, w_ref[:, lru_w:lru_w + fn_w], preferred_element_type=F32).astype(BF16)
    g0 = lru_w + fn_w
    for c0 in range(0, lru_w + fn_w, 512):
        g = jnp.dot(u, w_ref[:, g0 + c0:g0 + c0 + 512], preferred_element_type=F32)
        sg_ref[:, c0:c0 + 512] = _silu(g).astype(BF16)


def _in0(x, mod, w_bf, tm):
    b_, l_, d = x.shape
    lru_w, fn_w = d, d // 2
    mix = lru_w + fn_w
    return pl.pallas_call(
        functools.partial(_in0_kernel, lru_w=lru_w, fn_w=fn_w),
        grid=(b_, l_ // tm),
        in_specs=[
            pl.BlockSpec((None, tm, d), lambda b, i: (b, i, 0)),
            pl.BlockSpec((None, 3, d), lambda b, i: (b, 0, 0)),
            pl.BlockSpec((d, 2 * mix), lambda b, i: (0, 0)),
        ],
        out_specs=[
            pl.BlockSpec((None, tm, lru_w), lambda b, i: (b, i, 0)),
            pl.BlockSpec((None, tm, fn_w), lambda b, i: (b, i, 0)),
            pl.BlockSpec((None, tm, mix), lambda b, i: (b, i, 0)),
        ],
        out_shape=[
            jax.ShapeDtypeStruct((b_, l_, lru_w), BF16),
            jax.ShapeDtypeStruct((b_, l_, fn_w), BF16),
            jax.ShapeDtypeStruct((b_, l_, mix), BF16),
        ],
        compiler_params=_cparams(("parallel", "parallel")),
        name="in_proj0",
    )(x, mod, w_bf)


def _scan_kernel(z_ref, zp_ref, zn_ref, cw_ref, cb_ref, wg_ref, gb_ref, lam_ref, h0_ref,
                 h_ref, hfin_ref, zz_sc, a_sc, b_sc, hs_sc, carry_sc, *, reverse, t_rows):
    i = pl.program_id(1)
    nt = pl.num_programs(1)
    tt = (nt - 1 - i) if reverse else i
    halo = BF16_ROWS

    @pl.when(i == 0)
    def _():
        carry_sc[...] = jnp.broadcast_to(h0_ref[...], carry_sc.shape)

    zz_sc[0:halo, :] = jnp.where(tt > 0, zp_ref[...].astype(F32), 0.0)
    zz_sc[halo:halo + t_rows, :] = z_ref[...].astype(F32)
    zz_sc[halo + t_rows:halo + t_rows + halo, :] = jnp.where(tt < nt - 1, zn_ref[...].astype(F32), 0.0)

    width = z_ref.shape[-1]
    nl = -lam_ref[...]
    half_c = (0.5 * LRU_C) * (jnp.maximum(nl, 0.0) + jnp.log(1.0 + jnp.exp(-jnp.abs(nl))))
    half_gb = 0.5 * gb_ref[...]
    n_buf = t_rows + 2 * halo
    for j in range(width // LRU_GROUP):
        cs = slice(j * LRU_GROUP, (j + 1) * LRU_GROUP)
        zf = zz_sc[:, cs]
        xc = cb_ref[:, cs] + cw_ref[1:2, cs] * zf[halo:halo + t_rows]
        for k, shift in ((0, 1), (2, n_buf - 1), (3, n_buf - 2)):
            xc = xc + cw_ref[k:k + 1, cs] * pltpu.roll(zf, shift, axis=0)[halo:halo + t_rows]
        g = jnp.dot(xc.astype(BF16), wg_ref[j], preferred_element_type=F32)
        t_r = jnp.tanh(g[:, :LRU_GROUP] + half_gb[0:1, cs])
        t_i = jnp.tanh(g[:, LRU_GROUP:] + half_gb[1:2, cs])
        hc = half_c[:, cs]
        nla = hc * t_r + hc
        a = jnp.exp2(nla * (-LOG2E))
        y = jnp.tanh(nla) * (1.0 + a * a)
        sq = jnp.where(y > 0.0, y * lax.rsqrt(y), 0.0)
        hx = 0.5 * xc
        a_sc[:, cs] = a
        b_sc[:, cs] = sq * (hx * t_i + hx)

    def group(gi, h):
        base = (t_rows - SUBLANES - gi * SUBLANES) if reverse else gi * SUBLANES
        base = pl.multiple_of(base, SUBLANES)
        for t in range(SUBLANES):
            row = base + ((SUBLANES - 1 - t) if reverse else t)
            h = a_sc[pl.ds(row, 1), :] * h + b_sc[pl.ds(row, 1), :]
            hs_sc[pl.ds(row, 1), :] = h
        return h

    h_last = lax.fori_loop(0, t_rows // SUBLANES, group, carry_sc[0:1, :])
    carry_sc[0:1, :] = h_last
    h_ref[...] = hs_sc[...].astype(BF16)

    @pl.when(i == nt - 1)
    def _():
        hfin_ref[...] = h_last


def _lru_scan(zl, conv_w, conv_b, wg_d, gb_d, lam_d, h0_d, *, reverse, t_rows):
    b_, l_, w = zl.shape
    nt = l_ // t_rows
    hb = t_rows // BF16_ROWS
    n_hb = l_ // BF16_ROWS

    def tile(i):
        return (nt - 1 - i) if reverse else i

    return pl.pallas_call(
        functools.partial(_scan_kernel, reverse=reverse, t_rows=t_rows),
        grid=(b_, nt),
        in_specs=[
            pl.BlockSpec((None, t_rows, w), lambda b, i: (b, tile(i), 0)),
            pl.BlockSpec((None, BF16_ROWS, w), lambda b, i: (b, jnp.maximum(tile(i) * hb - 1, 0), 0)),
            pl.BlockSpec((None, BF16_ROWS, w), lambda b, i: (b, jnp.minimum((tile(i) + 1) * hb, n_hb - 1), 0)),
            pl.BlockSpec((CONV_W, w), lambda b, i: (0, 0)),
            pl.BlockSpec((1, w), lambda b, i: (0, 0)),
            pl.BlockSpec((w // LRU_GROUP, LRU_GROUP, 2 * LRU_GROUP), lambda b, i: (0, 0, 0)),
            pl.BlockSpec((2, w), lambda b, i: (0, 0)),
            pl.BlockSpec((1, w), lambda b, i: (0, 0)),
            pl.BlockSpec((None, 1, w), lambda b, i: (b, 0, 0)),
        ],
        out_specs=[
            pl.BlockSpec((None, t_rows, w), lambda b, i: (b, tile(i), 0)),
            pl.BlockSpec((None, 1, w), lambda b, i: (b, 0, 0)),
        ],
        out_shape=[
            jax.ShapeDtypeStruct((b_, l_, w), BF16),
            jax.ShapeDtypeStruct((b_, 1, w), F32),
        ],
        scratch_shapes=[
            pltpu.VMEM((t_rows + 2 * BF16_ROWS, w), F32),
            pltpu.VMEM((t_rows, w), F32),
            pltpu.VMEM((t_rows, w), F32),
            pltpu.VMEM((t_rows, w), F32),
            pltpu.VMEM((SUBLANES, w), F32),
        ],
        compiler_params=_cparams(("parallel", "arbitrary")),
        name="lru_scan_bwd" if reverse else "lru_scan_fwd",
    )(zl, zl, zl, conv_w, conv_b, wg_d, gb_d, lam_d, h0_d)


def _dft_cos_sin(n):
    k = np.arange(n, dtype=np.int64)
    ang = 2.0 * np.pi * ((k[:, None] * k[None, :]) % n).astype(np.float64) / n
    return np.cos(ang), np.sin(ang)


def _mxu_const(a):
    return jnp.asarray(a, F32).astype(BF16)


def _fnet_lat_kernel(x_ref, fc_ref, m1_ref, c2s2_ref, twa_ref, twb_ref, fw_ref, fb_ref, o_ref, s_sc,
                     *, n1, norm):
    n2 = FFT_N2
    pitch = FFT_PITCH
    gd = x_ref.shape[-1]

    nb = FFT_BATCH

    def stage0(i, carry):
        r0 = pl.multiple_of(i * (nb * n2), nb * n2)
        u = jnp.dot(x_ref[pl.ds(r0, nb * n2), :], fc_ref[...], preferred_element_type=F32)
        for j in range(nb):
            d0 = pl.multiple_of((i * nb + j) * pitch, SUBLANES)
            s_sc[0, pl.ds(d0, n2), :] = u[j * n2:(j + 1) * n2, :gd]
            s_sc[1, pl.ds(d0, n2), :] = u[j * n2:(j + 1) * n2, gd:]
        return carry

    lax.fori_loop(0, n1 // nb, stage0, 0)

    def stage1(a, carry):
        ta_r = twa_ref[0, a]
        ta_i = twa_ref[1, a]
        gs = []
        for b in range(SUBLANES):
            col = a * SUBLANES + b
            g_r = s_sc[0, pl.ds(col, n1, stride=pitch), :]
            g_i = s_sc[1, pl.ds(col, n1, stride=pitch), :]
            gs.append(jnp.concatenate([g_r, g_i], axis=0).astype(BF16))
        y_all = jnp.dot(m1_ref[...], jnp.concatenate(gs, axis=1), preferred_element_type=F32)
        for b in range(SUBLANES):
            col = a * SUBLANES + b
            y_r = y_all[:n1, b * gd:(b + 1) * gd]
            y_i = y_all[n1:, b * gd:(b + 1) * gd]
            tb_r = twb_ref[0, b]
            tb_i = twb_ref[1, b]
            c = ta_r * tb_r - ta_i * tb_i
            s = -(ta_r * tb_i + ta_i * tb_r)
            s_sc[0, pl.ds(col, n1, stride=pitch), :] = y_r * c + y_i * s
            s_sc[1, pl.ds(col, n1, stride=pitch), :] = y_i * c - y_r * s
        return carry

    lax.fori_loop(0, n2 // SUBLANES, stage1, 0)

    def stage2(i, carry):
        zs = []
        for j in range(nb):
            d0 = pl.multiple_of((i * nb + j) * pitch, SUBLANES)
            zs.append(jnp.concatenate([s_sc[0, pl.ds(d0, n2), :], s_sc[1, pl.ds(d0, n2), :]],
                                      axis=0).astype(BF16))
        z = jnp.concatenate(zs, axis=1)
        xr = jnp.dot(c2s2_ref[...], z, preferred_element_type=F32) * norm
        xs = jnp.concatenate([xr[:, j * gd:(j + 1) * gd] for j in range(nb)], axis=0).astype(BF16)
        out = jnp.dot(xs, fw_ref[...], preferred_element_type=F32) + fb_ref[...]
        for j in range(nb):
            o_ref[pl.ds(i * nb + j, n2, stride=n1), :] = out[j * n2:(j + 1) * n2, :]
        return carry

    lax.fori_loop(0, n1 // nb, stage2, 0)


def _fnet_lat(zf, fnet_w_bf, fnet_b):
    b_, l_, fw = zf.shape
    gd = FNET_GROUP_DIM
    groups = fw // gd
    n2 = FFT_N2
    n1 = l_ // n2
    assert n1 * n2 == l_ and n1 % SUBLANES == 0
    cc, sc = _dft_cos_sin(gd)
    fc = _mxu_const(np.concatenate([cc, -sc], axis=1))
    c1, s1 = _dft_cos_sin(n1)
    m1 = _mxu_const(np.block([[c1, s1], [-s1, c1]]))
    c2, s2 = _dft_cos_sin(n2)
    c2s2 = _mxu_const(np.concatenate([c2, s2], axis=1))
    k1 = np.arange(n1, dtype=np.float64)[None, :, None]
    ang_a = 2.0 * np.pi * (np.arange(n2 // SUBLANES, dtype=np.float64) * SUBLANES)[:, None, None] * k1 / l_
    ang_b = 2.0 * np.pi * np.arange(SUBLANES, dtype=np.float64)[:, None, None] * k1 / l_
    ones = np.ones((1, 1, gd))
    twa = jnp.asarray(np.stack([np.cos(ang_a) * ones, -np.sin(ang_a) * ones]), F32)
    twb = jnp.asarray(np.stack([np.cos(ang_b) * ones, -np.sin(ang_b) * ones]), F32)
    norm = 1.0 / math.sqrt(l_ * gd)
    const2 = lambda b, g: (0, 0)
    const4 = lambda b, g: (0, 0, 0, 0)
    return pl.pallas_call(
        functools.partial(_fnet_lat_kernel, n1=n1, norm=norm),
        grid=(b_, groups),
        in_specs=[
            pl.BlockSpec((None, l_, gd), lambda b, g: (b, 0, g)),
            pl.BlockSpec(fc.shape, const2),
            pl.BlockSpec(m1.shape, const2),
            pl.BlockSpec(c2s2.shape, const2),
            pl.BlockSpec(twa.shape, const4),
            pl.BlockSpec(twb.shape, const4),
            pl.BlockSpec((None, gd, gd), lambda b, g: (g, 0, 0)),
            pl.BlockSpec((None, 1, gd), lambda b, g: (g, 0, 0)),
        ],
        out_specs=pl.BlockSpec((None, l_, gd), lambda b, g: (b, 0, g)),
        out_shape=jax.ShapeDtypeStruct((b_, l_, fw), F32),
        scratch_shapes=[pltpu.VMEM((2, n1 * FFT_PITCH, gd), F32)],
        compiler_params=_cparams(("parallel", "parallel")),
        name="fnet_lat",
    )(zf, fc, m1, c2s2, twa, twb, fnet_w_bf, fnet_b.reshape(groups, 1, gd))


def _fnet_ctx_kernel(x_ref, fc_ref, cs_ref, fw_ref, fb_ref, o_ref, *, norm):
    gd = x_ref.shape[-1]
    u = jnp.dot(x_ref[...], fc_ref[...], preferred_element_type=F32)
    z = jnp.concatenate([u[:, :gd], u[:, gd:]], axis=0).astype(BF16)
    xr = jnp.dot(cs_ref[...], z, preferred_element_type=F32) * norm
    o_ref[...] = jnp.dot(xr.astype(BF16), fw_ref[...], preferred_element_type=F32) + fb_ref[...]


def _fnet_ctx(zf, fnet_w_bf, fnet_b):
    b_, l_, fw = zf.shape
    gd = FNET_GROUP_DIM
    groups = fw // gd
    cc, sc = _dft_cos_sin(gd)
    fc = _mxu_const(np.concatenate([cc, -sc], axis=1))
    cl, sl = _dft_cos_sin(l_)
    cs = _mxu_const(np.concatenate([cl, sl], axis=1))
    norm = 1.0 / math.sqrt(l_ * gd)
    const2 = lambda b, g: (0, 0)
    return pl.pallas_call(
        functools.partial(_fnet_ctx_kernel, norm=norm),
        grid=(b_, groups),
        in_specs=[
            pl.BlockSpec((None, l_, gd), lambda b, g: (b, 0, g)),
            pl.BlockSpec(fc.shape, const2),
            pl.BlockSpec(cs.shape, const2),
            pl.BlockSpec((None, gd, gd), lambda b, g: (g, 0, 0)),
            pl.BlockSpec((None, 1, gd), lambda b, g: (g, 0, 0)),
        ],
        out_specs=pl.BlockSpec((None, l_, gd), lambda b, g: (b, 0, g)),
        out_shape=jax.ShapeDtypeStruct((b_, l_, fw), F32),
        compiler_params=_cparams(("parallel", "parallel")),
        name="fnet_ctx",
    )(zf, fc, cs, fnet_w_bf, fnet_b.reshape(groups, 1, gd))


def _deepnorm(resid, gate, y, g, b):
    v = DEEPNORM_ALPHA * resid + gate * y
    mu = jnp.mean(v, axis=-1, keepdims=True)
    vc = v - mu
    var = jnp.mean(vc * vc, axis=-1, keepdims=True)
    return vc * lax.rsqrt(var + LN_EPS) * g + b


def _out0_kernel(hf_ref, hb_ref, f_ref, sg_ref, x_ref, mod_ref, w_ref, lg_ref, lb_ref, o_ref, *, lru_w):
    r = hf_ref[...].astype(F32) + hb_ref[...].astype(F32)
    m_l = (r * sg_ref[:, :lru_w].astype(F32)).astype(BF16)
    m_f = (f_ref[...] * sg_ref[:, lru_w:].astype(F32)).astype(BF16)
    y = jnp.dot(m_l, w_ref[:lru_w, :], preferred_element_type=F32)
    y = y + jnp.dot(m_f, w_ref[lru_w:, :], preferred_element_type=F32)
    o_ref[...] = _deepnorm(x_ref[...], mod_ref[2:3, :], y, lg_ref[...], lb_ref[...])


def _out0(hf, hb, f, sg, x, mod, w_bf, ln_g, ln_b, tm):
    b_, l_, d = x.shape
    lru_w = hf.shape[-1]
    fn_w = f.shape[-1]
    mix = lru_w + fn_w
    row = lambda b, i: (b, i, 0)
    const2 = lambda b, i: (0, 0)
    return pl.pallas_call(
        functools.partial(_out0_kernel, lru_w=lru_w),
        grid=(b_, l_ // tm),
        in_specs=[
            pl.BlockSpec((None, tm, lru_w), row),
            pl.BlockSpec((None, tm, lru_w), row),
            pl.BlockSpec((None, tm, fn_w), row),
            pl.BlockSpec((None, tm, mix), row),
            pl.BlockSpec((None, tm, d), row),
            pl.BlockSpec((None, 3, d), lambda b, i: (b, 0, 0)),
            pl.BlockSpec((mix, d), const2),
            pl.BlockSpec((1, d), const2),
            pl.BlockSpec((1, d), const2),
        ],
        out_specs=pl.BlockSpec((None, tm, d), row),
        out_shape=jax.ShapeDtypeStruct((b_, l_, d), F32),
        compiler_params=_cparams(("parallel", "parallel")),
        name="out_proj0",
    )(hf, hb, f, sg, x, mod, w_bf, ln_g, ln_b)


def _out1_kernel(o_ref_in, sg_ref, x_ref, mod_ref, w_ref, lg_ref, lb_ref, o_ref):
    m = (o_ref_in[...].astype(F32) * sg_ref[...].astype(F32)).astype(BF16)
    y = jnp.dot(m, w_ref[...], preferred_element_type=F32)
    o_ref[...] = _deepnorm(x_ref[...], mod_ref[2:3, :], y, lg_ref[...], lb_ref[...])


def _out1(o, sg, x, mod, w_bf, ln_g, ln_b, tm):
    b_, l_, d = x.shape
    wdt = o.shape[-1]
    row = lambda b, i: (b, i, 0)
    const2 = lambda b, i: (0, 0)
    return pl.pallas_call(
        _out1_kernel,
        grid=(b_, l_ // tm),
        in_specs=[
            pl.BlockSpec((None, tm, wdt), row),
            pl.BlockSpec((None, tm, wdt), row),
            pl.BlockSpec((None, tm, d), row),
            pl.BlockSpec((None, 3, d), lambda b, i: (b, 0, 0)),
            pl.BlockSpec((wdt, d), const2),
            pl.BlockSpec((1, d), const2),
            pl.BlockSpec((1, d), const2),
        ],
        out_specs=pl.BlockSpec((None, tm, d), row),
        out_shape=jax.ShapeDtypeStruct((b_, l_, d), F32),
        compiler_params=_cparams(("parallel", "parallel")),
        name="out_proj1",
    )(o, sg, x, mod, w_bf, ln_g, ln_b)


def _rms(x, g):
    return x * lax.rsqrt(jnp.mean(x * x, axis=-1, keepdims=True) + RMS_EPS) * g


def _in1_kernel(*refs, rope, want_q):
    if want_q:
        (x_ref, mod_ref, w_ref, qg_ref, kg_ref, wq_ref, wqs_ref, wk_ref, wvt_ref, cos_ref, sin_ref,
         q_ref, k_ref, vt_ref, sg_ref) = refs
    else:
        (x_ref, mod_ref, w_ref, kg_ref, wk_ref, wvt_ref, k_ref, vt_ref) = refs
    shift = mod_ref[0:1, :]
    scale = mod_ref[1:2, :]
    u = (x_ref[...] * (1.0 + scale) + shift).astype(BF16)
    hp = HEAD_PAD
    o_kv = Q_LORA
    o_kr = Q_LORA + KV_LORA
    o_g = o_kr + 2 * hp

    kvn = _rms(jnp.dot(u, w_ref[:, o_kv:o_kv + KV_LORA], preferred_element_type=F32), kg_ref[...]).astype(BF16)
    kr = jnp.dot(u, w_ref[:, o_kr:o_kr + 2 * hp], preferred_element_type=F32)
    if rope:
        cos = cos_ref[...]
        sin = sin_ref[...]
        krope = kr[:, :hp] * cos + kr[:, hp:] * sin
    else:
        krope = kr[:, :hp]
    for h2 in range(MLA_HEADS // 2):
        cs = slice(2 * h2 * hp, (2 * h2 + 2) * hp)
        kn = jnp.dot(kvn, wk_ref[:, cs], preferred_element_type=F32)
        k_ref[2 * h2] = (kn[:, :hp] + krope).astype(BF16)
        k_ref[2 * h2 + 1] = (kn[:, hp:] + krope).astype(BF16)
    vt = lax.dot_general(wvt_ref[...], kvn, (((1,), (1,)), ((), ())), preferred_element_type=F32)
    row = lax.broadcasted_iota(jnp.int32, (vt.shape[0], 1), 0)
    vt_ref[...] = (vt + (row % VT_ROWS == V_DIM).astype(F32)).astype(BF16)

    if want_q:
        qn = _rms(jnp.dot(u, w_ref[:, :Q_LORA], preferred_element_type=F32), qg_ref[...]).astype(BF16)
        qscale = ATTN_SCALE * LOG2E
        cq = cos * qscale
        sq = sin * qscale
        for h2 in range(MLA_HEADS // 2):
            cs = slice(2 * h2 * hp, (2 * h2 + 2) * hp)
            z1 = jnp.dot(qn, wq_ref[:, cs], preferred_element_type=F32)
            z2 = jnp.dot(qn, wqs_ref[:, cs], preferred_element_type=F32)
            q_ref[2 * h2] = (z1[:, :hp] * cq + z2[:, :hp] * sq).astype(BF16)
            q_ref[2 * h2 + 1] = (z1[:, hp:] * cq + z2[:, hp:] * sq).astype(BF16)
        gw = sg_ref.shape[-1]
        for c0 in range(0, gw, 512):
            g = jnp.dot(u, w_ref[:, o_g + c0:o_g + c0 + 512], preferred_element_type=F32)
            sg_ref[:, c0:c0 + 512] = _silu(g).astype(BF16)


def _in1(x, mod, w_in_p, qg, kg, wq_p, wqs_p, wk_p, wvt_p, cos_t, sin_t, tm, *, want_q):
    b_, l_, d = x.shape
    hp = HEAD_PAD
    nh = MLA_HEADS
    gw = nh * V_DIM
    row = lambda b, i: (b, i, 0)
    const2 = lambda b, i: (0, 0)
    head = lambda b, i: (b, 0, i, 0)
    kv_specs = [pl.BlockSpec((None, nh, tm, hp), head),
                pl.BlockSpec((None, nh * VT_ROWS, tm), lambda b, i: (b, 0, i))]
    kv_shapes = [jax.ShapeDtypeStruct((b_, nh, l_, hp), BF16),
                 jax.ShapeDtypeStruct((b_, nh * VT_ROWS, l_), BF16)]
    if want_q:
        ins = [x, mod, w_in_p, qg, kg, wq_p, wqs_p, wk_p, wvt_p, cos_t, sin_t]
        in_specs = [
            pl.BlockSpec((None, tm, d), row),
            pl.BlockSpec((None, 3, d), lambda b, i: (b, 0, 0)),
            pl.BlockSpec(w_in_p.shape, const2),
            pl.BlockSpec(qg.shape, const2),
            pl.BlockSpec(kg.shape, const2),
            pl.BlockSpec(wq_p.shape, const2),
            pl.BlockSpec(wqs_p.shape, const2),
            pl.BlockSpec(wk_p.shape, const2),
            pl.BlockSpec(wvt_p.shape, const2),
            pl.BlockSpec((tm, hp), lambda b, i: (i, 0)),
            pl.BlockSpec((tm, hp), lambda b, i: (i, 0)),
        ]
        out_specs = [pl.BlockSpec((None, nh, tm, hp), head)] + kv_specs + [pl.BlockSpec((None, tm, gw), row)]
        out_shape = [jax.ShapeDtypeStruct((b_, nh, l_, hp), BF16)] + kv_shapes + [
            jax.ShapeDtypeStruct((b_, l_, gw), BF16)]
    else:
        ins = [x, mod, w_in_p, kg, wk_p, wvt_p]
        in_specs = [
            pl.BlockSpec((None, tm, d), row),
            pl.BlockSpec((None, 3, d), lambda b, i: (b, 0, 0)),
            pl.BlockSpec(w_in_p.shape, const2),
            pl.BlockSpec(kg.shape, const2),
            pl.BlockSpec(wk_p.shape, const2),
            pl.BlockSpec(wvt_p.shape, const2),
        ]
        out_specs = kv_specs
        out_shape = kv_shapes
    return pl.pallas_call(
        functools.partial(_in1_kernel, rope=want_q, want_q=want_q),
        grid=(b_, l_ // tm),
        in_specs=in_specs,
        out_specs=out_specs,
        out_shape=out_shape,
        compiler_params=_cparams(("parallel", "parallel")),
        name="in_proj1" if want_q else "in_proj1_ctx",
    )(*ins)


def _attn_kernel(q_ref, kc_ref, vtc_ref, k_ref, vt_ref, o_ref, m_sc, acc_sc, s_sc):
    sub = ATTN_SUB
    n_sub = k_ref.shape[1] // sub

    m_sc[...] = jnp.full(m_sc.shape, NEG_BIG, F32)
    acc_sc[...] = jnp.zeros(acc_sc.shape, F32)

    def scores(slot, k_fn):
        for hh in range(2):
            s_sc[slot, hh] = lax.dot_general(k_fn(hh), q_ref[hh], (((1,), (1,)), ((), ())),
                                             preferred_element_type=F32)

    def update(hh, s, vt):
        m_prev = m_sc[hh]
        m_new = jnp.maximum(m_prev, jnp.max(s, axis=0, keepdims=True))
        alpha = jnp.exp2(m_prev - m_new)
        p = jnp.exp2(s - m_new).astype(BF16)
        acc_sc[hh] = alpha * acc_sc[hh] + jnp.dot(vt, p, preferred_element_type=F32)
        m_sc[hh] = m_new

    def softmax_pv(slot, vt_fn):
        for hh in range(2):
            update(hh, s_sc[slot, hh], vt_fn(hh))

    def k_at(j):
        o = pl.multiple_of(j * sub, sub)
        return lambda hh: k_ref[hh, pl.ds(o, sub), :]

    def vt_at(j):
        o = pl.multiple_of(j * sub, sub)
        return lambda hh: vt_ref[hh * VT_ROWS:(hh + 1) * VT_ROWS, pl.ds(o, sub)]

    scores(0, k_at(0))
    for hh in range(2):
        s = lax.dot_general(kc_ref[hh], q_ref[hh], (((1,), (1,)), ((), ())), preferred_element_type=F32)
        update(hh, s, vtc_ref[hh * VT_ROWS:(hh + 1) * VT_ROWS, :])

    def body(i, carry):
        j = 2 * i
        scores(1, k_at(j + 1))
        softmax_pv(0, vt_at(j))
        scores(0, k_at(jnp.minimum(j + 2, n_sub - 1)))
        softmax_pv(1, vt_at(j + 1))
        return carry

    lax.fori_loop(0, n_sub // 2, body, 0, unroll=math.gcd(ATTN_UNROLL, n_sub // 2))

    halves = []
    for hh in range(2):
        acc = acc_sc[hh]
        halves.append(acc[:V_DIM] * (1.0 / acc[V_DIM:V_DIM + 1]))
    o_ref[...] = jnp.concatenate(halves, axis=0).T.astype(BF16)


def _attention(q, k_ctx, vt_ctx, k_lat, vt_lat, tq):
    b_, nh, s_, hp = q.shape
    c_ = k_ctx.shape[2]
    assert s_ % (2 * ATTN_SUB) == 0
    return pl.pallas_call(
        _attn_kernel,
        grid=(b_, nh // 2, s_ // tq),
        in_specs=[
            pl.BlockSpec((None, 2, tq, hp), lambda b, h, i: (b, h, i, 0)),
            pl.BlockSpec((None, 2, c_, hp), lambda b, h, i: (b, h, 0, 0)),
            pl.BlockSpec((None, 2 * VT_ROWS, c_), lambda b, h, i: (b, h, 0)),
            pl.BlockSpec((None, 2, s_, hp), lambda b, h, i: (b, h, 0, 0)),
            pl.BlockSpec((None, 2 * VT_ROWS, s_), lambda b, h, i: (b, h, 0)),
        ],
        out_specs=pl.BlockSpec((None, tq, 2 * V_DIM), lambda b, h, i: (b, i, h)),
        out_shape=jax.ShapeDtypeStruct((b_, s_, nh * V_DIM), BF16),
        scratch_shapes=[pltpu.VMEM((2, 1, tq), F32), pltpu.VMEM((2, VT_ROWS, tq), F32),
                        pltpu.VMEM((2, 2, ATTN_SUB, tq), F32)],
        compiler_params=_cparams(("parallel", "parallel", "arbitrary")),
        name="mla_attention",
    )(q, k_ctx, vt_ctx, k_lat, vt_lat)


def _gate_blocks(gate_w_d):
    hd = LRU_HEAD_DIM
    per = LRU_GROUP // hd
    ng = LRU_HEADS // per
    w = gate_w_d.reshape(2, ng, per, hd, hd)
    eye = jnp.eye(per, dtype=gate_w_d.dtype)
    blk = jnp.einsum('gnpij,pq->gnpiqj', w, eye).reshape(2, ng, LRU_GROUP, LRU_GROUP)
    return (0.5 * jnp.concatenate([blk[0], blk[1]], axis=-1)).astype(BF16)


def _pad_heads(w, per_head, offset=0):
    k = w.shape[0]
    w3 = w.reshape(k, MLA_HEADS, per_head)
    out = jnp.zeros((k, MLA_HEADS, HEAD_PAD), w.dtype)
    out = out.at[:, :, offset:offset + per_head].set(w3)
    return out.reshape(k, MLA_HEADS * HEAD_PAD)


def _rope_swap_idx():
    return np.arange(QK_ROPE) ^ ROPE_PAIRS


def _prep_mla(w_in, w_uq, w_ukv):
    d = w_in.shape[0]
    qc = w_in[:, :Q_LORA]
    kvc = w_in[:, Q_LORA:Q_LORA + KV_LORA]
    kr = w_in[:, Q_LORA + KV_LORA:Q_LORA + KV_LORA + QK_ROPE]
    g = w_in[:, Q_LORA + KV_LORA + QK_ROPE:]
    swap = _rope_swap_idx()
    kr_blk = jnp.zeros((d, HEAD_PAD), w_in.dtype).at[:, QK_NOPE:QK_NOPE + QK_ROPE].set(kr)
    krs_blk = jnp.zeros((d, HEAD_PAD), w_in.dtype).at[:, QK_NOPE:QK_NOPE + QK_ROPE].set(kr[:, swap])
    w_in_p = jnp.concatenate([qc, kvc, kr_blk, krs_blk, g], axis=1).astype(BF16)

    wq3 = w_uq.reshape(Q_LORA, MLA_HEADS, QK_DIM)
    wq_p = _pad_heads(w_uq, QK_DIM).astype(BF16)
    rope_sw = wq3[:, :, QK_NOPE:][:, :, swap]
    wqs = jnp.zeros((Q_LORA, MLA_HEADS, HEAD_PAD), w_uq.dtype).at[:, :, QK_NOPE:QK_NOPE + QK_ROPE].set(rope_sw)
    wqs_p = wqs.reshape(Q_LORA, MLA_HEADS * HEAD_PAD).astype(BF16)

    wkv3 = w_ukv.reshape(KV_LORA, MLA_HEADS, QK_NOPE + V_DIM)
    wk_p = _pad_heads(wkv3[:, :, :QK_NOPE].reshape(KV_LORA, -1), QK_NOPE).astype(BF16)
    wv3 = jnp.transpose(wkv3[:, :, QK_NOPE:], (1, 2, 0))
    wvt_p = jnp.pad(wv3, ((0, 0), (0, VT_ROWS - V_DIM), (0, 0))).reshape(MLA_HEADS * VT_ROWS, KV_LORA)
    return w_in_p, wq_p, wqs_p, wk_p, wvt_p.astype(BF16)


def _rope_tables(s_):
    rows = s_ // GRID_W
    inv = ROPE_THETA ** (-jnp.arange(ROPE_PAIRS, dtype=F32) / ROPE_PAIRS)
    a_r = jnp.arange(rows, dtype=F32)[:, None] * inv
    a_c = jnp.arange(GRID_W, dtype=F32)[:, None] * inv
    shape = (rows, GRID_W, ROPE_PAIRS)
    by_row = lambda t: jnp.broadcast_to(t[:, None, :], shape)
    by_col = lambda t: jnp.broadcast_to(t[None, :, :], shape)
    cr, sr, cc, sc = by_row(jnp.cos(a_r)), by_row(jnp.sin(a_r)), by_col(jnp.cos(a_c)), by_col(jnp.sin(a_c))
    pad = HEAD_PAD - QK_NOPE - QK_ROPE
    cos_t = jnp.concatenate([jnp.ones(shape[:2] + (QK_NOPE,), F32), cr, cr, cc, cc,
                             jnp.ones(shape[:2] + (pad,), F32)], axis=-1)
    sin_t = jnp.concatenate([jnp.zeros(shape[:2] + (QK_NOPE,), F32), -sr, sr, -sc, sc,
                             jnp.zeros(shape[:2] + (pad,), F32)], axis=-1)
    return cos_t.reshape(s_, HEAD_PAD), sin_t.reshape(s_, HEAD_PAD)


def kernel(x, c, ctx, c_ctx, ada_w, ada_b, ln_g, ln_b, w_in_rf, conv_w, conv_b, lru_gate_w, lru_gate_b,
           lru_lambda, fnet_w, fnet_b, w_out_rf, w_in_mla, q_norm_g, kv_norm_g, w_uq, w_ukv, w_out_mla):
    b_, s_, d = x.shape
    c_len = ctx.shape[1]
    tm = 512 if s_ % 512 == 0 else s_
    t_scan = 512 if s_ % 512 == 0 else s_

    cond = jnp.zeros((SUBLANES, d), F32).at[:b_].set(c).at[b_].set(c_ctx)
    mod = _adaln(cond, ada_w, ada_b).reshape(DEPTH, SUBLANES, 3, d)
    mod_lat = [mod[l, :b_] for l in range(DEPTH)]
    mod_ctx = [jnp.broadcast_to(mod[l, b_][None], (b_, 3, d)) for l in range(DEPTH)]

    w_in0 = w_in_rf[0].astype(BF16)
    w_out0 = w_out_rf[0].astype(BF16)
    fw_bf = fnet_w[0].astype(BF16)
    wg = [_gate_blocks(lru_gate_w[0, dd]) for dd in range(2)]
    cb = conv_b[0].reshape(1, -1)

    zl_c, zf_c, sg_c = _in0(ctx, mod_ctx[0], w_in0, c_len)
    zl_l, zf_l, sg_l = _in0(x, mod_lat[0], w_in0, tm)

    h_c, h_l = [], []
    for dd in range(2):
        rev = dd == 1
        lam = lru_lambda[0, dd].reshape(1, -1)
        zero = jnp.zeros((b_, 1, d), F32)
        hc, hc_fin = _lru_scan(zl_c, conv_w[0], cb, wg[dd], lru_gate_b[0, dd], lam, zero,
                               reverse=rev, t_rows=c_len)
        hl, _ = _lru_scan(zl_l, conv_w[0], cb, wg[dd], lru_gate_b[0, dd], lam, hc_fin,
                          reverse=rev, t_rows=t_scan)
        h_c.append(hc)
        h_l.append(hl)

    f_c = _fnet_ctx(zf_c, fw_bf, fnet_b[0])
    f_l = _fnet_lat(zf_l, fw_bf, fnet_b[0])

    lg0, lb0 = ln_g[0].reshape(1, d), ln_b[0].reshape(1, d)
    h1_c = _out0(h_c[0], h_c[1], f_c, sg_c, ctx, mod_ctx[0], w_out0, lg0, lb0, c_len)
    h1_l = _out0(h_l[0], h_l[1], f_l, sg_l, x, mod_lat[0], w_out0, lg0, lb0, tm)

    w_in_p, wq_p, wqs_p, wk_p, wvt_p = _prep_mla(w_in_mla[0], w_uq[0], w_ukv[0])
    qg = q_norm_g[0].reshape(1, -1)
    kg = kv_norm_g[0].reshape(1, -1)
    cos_t, sin_t = _rope_tables(s_)

    k_c, v_c = _in1(h1_c, mod_ctx[1], w_in_p, qg, kg, wq_p, wqs_p, wk_p, wvt_p, cos_t, sin_t, c_len,
                    want_q=False)
    q_l, k_l, v_l, sg1 = _in1(h1_l, mod_lat[1], w_in_p, qg, kg, wq_p, wqs_p, wk_p, wvt_p, cos_t, sin_t, tm,
                              want_q=True)

    tq = 512 if s_ % 1024 == 0 else s_
    o = _attention(q_l, k_c, v_c, k_l, v_l, tq)

    lg1, lb1 = ln_g[1].reshape(1, d), ln_b[1].reshape(1, d)
    return _out1(o, sg1, h1_l, mod_lat[1], w_out_mla[0].astype(BF16), lg1, lb1, tm)
```

```python
import functools
import math

import numpy as np
import jax
import jax.numpy as jnp
from jax import lax
from jax.experimental import pallas as pl
from jax.experimental.pallas import tpu as pltpu

F32 = jnp.float32
BF16 = jnp.bfloat16

DEPTH = 2
GRID_W = 64
DEEPNORM_ALPHA = (2 * DEPTH) ** 0.25
LN_EPS = 1e-6
RMS_EPS = 1e-6
LRU_HEADS = 16
LRU_HEAD_DIM = 64
LRU_GROUP = 256
CONV_W = 4
LRU_C = 8.0
FNET_GROUPS = 4
FNET_GROUP_DIM = 128
MLA_HEADS = 16
Q_LORA = 256
KV_LORA = 128
QK_NOPE = 64
QK_ROPE = 32
V_DIM = 64
QK_DIM = QK_NOPE + QK_ROPE
ROPE_PAIRS = QK_ROPE // 4
ROPE_THETA = 10000.0
ATTN_SCALE = QK_DIM ** -0.5
LOG2E = 1.4426950408889634

LANES = 128
SUBLANES = 8
BF16_ROWS = 16
VMEM_LIMIT = 56 * 1024 * 1024

HEAD_PAD = LANES
VT_ROWS = V_DIM + BF16_ROWS
ATTN_SUB = 256
ATTN_UNROLL = 8
FFT_N2 = 128
FFT_PITCH = FFT_N2 + SUBLANES
FFT_BATCH = 4
NEG_BIG = -1e30


def _sigmoid(x):
    return 0.5 * (jnp.tanh(0.5 * x) + 1.0)


def _silu(x):
    return x * _sigmoid(x)


def _cparams(sem):
    return pltpu.CompilerParams(dimension_semantics=sem, vmem_limit_bytes=VMEM_LIMIT)


def _adaln_kernel(cond_ref, w_ref, b_ref, o_ref):
    c = cond_ref[...]
    o_ref[...] = jnp.dot(_silu(c), w_ref[...], preferred_element_type=F32,
                         precision=lax.Precision.HIGHEST) + b_ref[...]


def _adaln(cond, ada_w, ada_b):
    depth, d, n = ada_w.shape
    rows = cond.shape[0]
    tn = 768
    return pl.pallas_call(
        _adaln_kernel,
        grid=(depth, n // tn),
        in_specs=[
            pl.BlockSpec((rows, d), lambda l, j: (0, 0)),
            pl.BlockSpec((None, d, tn), lambda l, j: (l, 0, j)),
            pl.BlockSpec((None, 1, tn), lambda l, j: (l, 0, j)),
        ],
        out_specs=pl.BlockSpec((None, rows, tn), lambda l, j: (l, 0, j)),
        out_shape=jax.ShapeDtypeStruct((depth, rows, n), F32),
        compiler_params=_cparams(("parallel", "parallel")),
        name="adaln",
    )(cond, ada_w, ada_b.reshape(depth, 1, n))


def _in0_kernel(x_ref, mod_ref, w_ref, zl_ref, zf_ref, sg_ref, *, lru_w, fn_w):
    shift = mod_ref[0:1, :]
    scale = mod_ref[1:2, :]
    u = (x_ref[...] * (1.0 + scale) + shift).astype(BF16)
    zl_ref[...] = jnp.dot(u, w_ref[:, :lru_w], preferred_element_type=F32).astype(BF16)
    zf_ref[...] = jnp.dot(u, w_ref[:, lru_w:lru_w + fn_w], preferred_element_type=F32).astype(BF16)
    g0 = lru_w + fn_w
    for c0 in range(0, lru_w + fn_w, 512):
        g = jnp.dot(u, w_ref[:, g0 + c0:g0 + c0 + 512], preferred_element_type=F32)
        sg_ref[:, c0:c0 + 512] = _silu(g).astype(BF16)


def _in0(x, mod, w_bf, tm):
    b_, l_, d = x.shape
    lru_w, fn_w = d, d // 2
    mix = lru_w + fn_w
    return pl.pallas_call(
        functools.partial(_in0_kernel, lru_w=lru_w, fn_w=fn_w),
        grid=(b_, l_ // tm),
        in_specs=[
            pl.BlockSpec((None, tm, d), lambda b, i: (b, i, 0)),
            pl.BlockSpec((None, 3, d), lambda b, i: (b, 0, 0)),
            pl.BlockSpec((d, 2 * mix), lambda b, i: (0, 0)),
        ],
        out_specs=[
            pl.BlockSpec((None, tm, lru_w), lambda b, i: (b, i, 0)),
            pl.BlockSpec((None, tm, fn_w), lambda b, i: (b, i, 0)),
            pl.BlockSpec((None, tm, mix), lambda b, i: (b, i, 0)),
        ],
        out_shape=[
            jax.ShapeDtypeStruct((b_, l_, lru_w), BF16),
            jax.ShapeDtypeStruct((b_, l_, fn_w), BF16),
            jax.ShapeDtypeStruct((b_, l_, mix), BF16),
        ],
        compiler_params=_cparams(("parallel", "parallel")),
        name="in_proj0",
    )(x, mod, w_bf)


def _scan_kernel(z_ref, zp_ref, zn_ref, cw_ref, cb_ref, wg_ref, gb_ref, lam_ref, h0_ref,
                 h_ref, hfin_ref, zz_sc, a_sc, b_sc, hs_sc, carry_sc, *, reverse, t_rows):
    i = pl.program_id(1)
    nt = pl.num_programs(1)
    tt = (nt - 1 - i) if reverse else i
    halo = BF16_ROWS

    @pl.when(i == 0)
    def _():
        carry_sc[...] = jnp.broadcast_to(h0_ref[...], carry_sc.shape)

    zz_sc[0:halo, :] = jnp.where(tt > 0, zp_ref[...].astype(F32), 0.0)
    zz_sc[halo:halo + t_rows, :] = z_ref[...].astype(F32)
    zz_sc[halo + t_rows:halo + t_rows + halo, :] = jnp.where(tt < nt - 1, zn_ref[...].astype(F32), 0.0)

    width = z_ref.shape[-1]
    nl = -lam_ref[...]
    half_c = (0.5 * LRU_C) * (jnp.maximum(nl, 0.0) + jnp.log(1.0 + jnp.exp(-jnp.abs(nl))))
    half_gb = 0.5 * gb_ref[...]
    n_buf = t_rows + 2 * halo
    for j in range(width // LRU_GROUP):
        cs = slice(j * LRU_GROUP, (j + 1) * LRU_GROUP)
        zf = zz_sc[:, cs]
        xc = cb_ref[:, cs] + cw_ref[1:2, cs] * zf[halo:halo + t_rows]
        for k, shift in ((0, 1), (2, n_buf - 1), (3, n_buf - 2)):
            xc = xc + cw_ref[k:k + 1, cs] * pltpu.roll(zf, shift, axis=0)[halo:halo + t_rows]
        g = jnp.dot(xc.astype(BF16), wg_ref[j], preferred_element_type=F32)
        t_r = jnp.tanh(g[:, :LRU_GROUP] + half_gb[0:1, cs])
        t_i = jnp.tanh(g[:, LRU_GROUP:] + half_gb[1:2, cs])
        hc = half_c[:, cs]
        nla = hc * t_r + hc
        a = jnp.exp2(nla * (-LOG2E))
        y = jnp.tanh(nla) * (1.0 + a * a)
        sq = jnp.where(y > 0.0, y * lax.rsqrt(y), 0.0)
        hx = 0.5 * xc
        a_sc[:, cs] = a
        b_sc[:, cs] = sq * (hx * t_i + hx)

    def group(gi, h):
        base = (t_rows - SUBLANES - gi * SUBLANES) if reverse else gi * SUBLANES
        base = pl.multiple_of(base, SUBLANES)
        for t in range(SUBLANES):
            row = base + ((SUBLANES - 1 - t) if reverse else t)
            h = a_sc[pl.ds(row, 1), :] * h + b_sc[pl.ds(row, 1), :]
            hs_sc[pl.ds(row, 1), :] = h
        return h

    h_last = lax.fori_loop(0, t_rows // SUBLANES, group, carry_sc[0:1, :])
    carry_sc[0:1, :] = h_last
    h_ref[...] = hs_sc[...].astype(BF16)

    @pl.when(i == nt - 1)
    def _():
        hfin_ref[...] = h_last


def _lru_scan(zl, conv_w, conv_b, wg_d, gb_d, lam_d, h0_d, *, reverse, t_rows):
    b_, l_, w = zl.shape
    nt = l_ // t_rows
    hb = t_rows // BF16_ROWS
    n_hb = l_ // BF16_ROWS

    def tile(i):
        return (nt - 1 - i) if reverse else i

    return pl.pallas_call(
        functools.partial(_scan_kernel, reverse=reverse, t_rows=t_rows),
        grid=(b_, nt),
        in_specs=[
            pl.BlockSpec((None, t_rows, w), lambda b, i: (b, tile(i), 0)),
            pl.BlockSpec((None, BF16_ROWS, w), lambda b, i: (b, jnp.maximum(tile(i) * hb - 1, 0), 0)),
            pl.BlockSpec((None, BF16_ROWS, w), lambda b, i: (b, jnp.minimum((tile(i) + 1) * hb, n_hb - 1), 0)),
            pl.BlockSpec((CONV_W, w), lambda b, i: (0, 0)),
            pl.BlockSpec((1, w), lambda b, i: (0, 0)),
            pl.BlockSpec((w // LRU_GROUP, LRU_GROUP, 2 * LRU_GROUP), lambda b, i: (0, 0, 0)),
            pl.BlockSpec((2, w), lambda b, i: (0, 0)),
            pl.BlockSpec((1, w), lambda b, i: (0, 0)),
            pl.BlockSpec((None, 1, w), lambda b, i: (b, 0, 0)),
        ],
        out_specs=[
            pl.BlockSpec((None, t_rows, w), lambda b, i: (b, tile(i), 0)),
            pl.BlockSpec((None, 1, w), lambda b, i: (b, 0, 0)),
        ],
        out_shape=[
            jax.ShapeDtypeStruct((b_, l_, w), BF16),
            jax.ShapeDtypeStruct((b_, 1, w), F32),
        ],
        scratch_shapes=[
            pltpu.VMEM((t_rows + 2 * BF16_ROWS, w), F32),
            pltpu.VMEM((t_rows, w), F32),
            pltpu.VMEM((t_rows, w), F32),
            pltpu.VMEM((t_rows, w), F32),
            pltpu.VMEM((SUBLANES, w), F32),
        ],
        compiler_params=_cparams(("parallel", "arbitrary")),
        name="lru_scan_bwd" if reverse else "lru_scan_fwd",
    )(zl, zl, zl, conv_w, conv_b, wg_d, gb_d, lam_d, h0_d)


def _dft_cos_sin(n):
    k = np.arange(n, dtype=np.int64)
    ang = 2.0 * np.pi * ((k[:, None] * k[None, :]) % n).astype(np.float64) / n
    return np.cos(ang), np.sin(ang)


def _mxu_const(a):
    return jnp.asarray(a, F32).astype(BF16)


def _fnet_lat_kernel(x_ref, fc_ref, m1_ref, c2s2_ref, twa_ref, twb_ref, fw_ref, fb_ref, o_ref, s_sc,
                     *, n1, norm):
    n2 = FFT_N2
    pitch = FFT_PITCH
    gd = x_ref.shape[-1]

    nb = FFT_BATCH

    def stage0(i, carry):
        r0 = pl.multiple_of(i * (nb * n2), nb * n2)
        u = jnp.dot(x_ref[pl.ds(r0, nb * n2), :], fc_ref[...], preferred_element_type=F32)
        for j in range(nb):
            d0 = pl.multiple_of((i * nb + j) * pitch, SUBLANES)
            s_sc[0, pl.ds(d0, n2), :] = u[j * n2:(j + 1) * n2, :gd]
            s_sc[1, pl.ds(d0, n2), :] = u[j * n2:(j + 1) * n2, gd:]
        return carry

    lax.fori_loop(0, n1 // nb, stage0, 0)

    def stage1(a, carry):
        ta_r = twa_ref[0, a]
        ta_i = twa_ref[1, a]
        gs = []
        for b in range(SUBLANES):
            col = a * SUBLANES + b
            g_r = s_sc[0, pl.ds(col, n1, stride=pitch), :]
            g_i = s_sc[1, pl.ds(col, n1, stride=pitch), :]
            gs.append(jnp.concatenate([g_r, g_i], axis=0).astype(BF16))
        y_all = jnp.dot(m1_ref[...], jnp.concatenate(gs, axis=1), preferred_element_type=F32)
        for b in range(SUBLANES):
            col = a * SUBLANES + b
            y_r = y_all[:n1, b * gd:(b + 1) * gd]
            y_i = y_all[n1:, b * gd:(b + 1) * gd]
            tb_r = twb_ref[0, b]
            tb_i = twb_ref[1, b]
            c = ta_r * tb_r - ta_i * tb_i
            s = -(ta_r * tb_i + ta_i * tb_r)
            s_sc[0, pl.ds(col, n1, stride=pitch), :] = y_r * c + y_i * s
            s_sc[1, pl.ds(col, n1, stride=pitch), :] = y_i * c - y_r * s
        return carry

    lax.fori_loop(0, n2 // SUBLANES, stage1, 0)

    def stage2(i, carry):
        zs = []
        for j in range(nb):
            d0 = pl.multiple_of((i * nb + j) * pitch, SUBLANES)
            zs.append(jnp.concatenate([s_sc[0, pl.ds(d0, n2), :], s_sc[1, pl.ds(d0, n2), :]],
                                      axis=0).astype(BF16))
        z = jnp.concatenate(zs, axis=1)
        xr = jnp.dot(c2s2_ref[...], z, preferred_element_type=F32) * norm
        xs = jnp.concatenate([xr[:, j * gd:(j + 1) * gd] for j in range(nb)], axis=0).astype(BF16)
        out = jnp.dot(xs, fw_ref[...], preferred_element_type=F32) + fb_ref[...]
        for j in range(nb):
            o_ref[pl.ds(i * nb + j, n2, stride=n1), :] = out[j * n2:(j + 1) * n2, :]
        return carry

    lax.fori_loop(0, n1 // nb, stage2, 0)


def _fnet_lat(zf, fnet_w_bf, fnet_b):
    b_, l_, fw = zf.shape
    gd = FNET_GROUP_DIM
    groups = fw // gd
    n2 = FFT_N2
    n1 = l_ // n2
    assert n1 * n2 == l_ and n1 % SUBLANES == 0
    cc, sc = _dft_cos_sin(gd)
    fc = _mxu_const(np.concatenate([cc, -sc], axis=1))
    c1, s1 = _dft_cos_sin(n1)
    m1 = _mxu_const(np.block([[c1, s1], [-s1, c1]]))
    c2, s2 = _dft_cos_sin(n2)
    c2s2 = _mxu_const(np.concatenate([c2, s2], axis=1))
    k1 = np.arange(n1, dtype=np.float64)[None, :, None]
    ang_a = 2.0 * np.pi * (np.arange(n2 // SUBLANES, dtype=np.float64) * SUBLANES)[:, None, None] * k1 / l_
    ang_b = 2.0 * np.pi * np.arange(SUBLANES, dtype=np.float64)[:, None, None] * k1 / l_
    ones = np.ones((1, 1, gd))
    twa = jnp.asarray(np.stack([np.cos(ang_a) * ones, -np.sin(ang_a) * ones]), F32)
    twb = jnp.asarray(np.stack([np.cos(ang_b) * ones, -np.sin(ang_b) * ones]), F32)
    norm = 1.0 / math.sqrt(l_ * gd)
    const2 = lambda b, g: (0, 0)
    const4 = lambda b, g: (0, 0, 0, 0)
    return pl.pallas_call(
        functools.partial(_fnet_lat_kernel, n1=n1, norm=norm),
        grid=(b_, groups),
        in_specs=[
            pl.BlockSpec((None, l_, gd), lambda b, g: (b, 0, g)),
            pl.BlockSpec(fc.shape, const2),
            pl.BlockSpec(m1.shape, const2),
            pl.BlockSpec(c2s2.shape, const2),
            pl.BlockSpec(twa.shape, const4),
            pl.BlockSpec(twb.shape, const4),
            pl.BlockSpec((None, gd, gd), lambda b, g: (g, 0, 0)),
            pl.BlockSpec((None, 1, gd), lambda b, g: (g, 0, 0)),
        ],
        out_specs=pl.BlockSpec((None, l_, gd), lambda b, g: (b, 0, g)),
        out_shape=jax.ShapeDtypeStruct((b_, l_, fw), F32),
        scratch_shapes=[pltpu.VMEM((2, n1 * FFT_PITCH, gd), F32)],
        compiler_params=_cparams(("parallel", "parallel")),
        name="fnet_lat",
    )(zf, fc, m1, c2s2, twa, twb, fnet_w_bf, fnet_b.reshape(groups, 1, gd))


def _fnet_ctx_kernel(x_ref, fc_ref, cs_ref, fw_ref, fb_ref, o_ref, *, norm):
    gd = x_ref.shape[-1]
    u = jnp.dot(x_ref[...], fc_ref[...], preferred_element_type=F32)
    z = jnp.concatenate([u[:, :gd], u[:, gd:]], axis=0).astype(BF16)
    xr = jnp.dot(cs_ref[...], z, preferred_element_type=F32) * norm
    o_ref[...] = jnp.dot(xr.astype(BF16), fw_ref[...], preferred_element_type=F32) + fb_ref[...]


def _fnet_ctx(zf, fnet_w_bf, fnet_b):
    b_, l_, fw = zf.shape
    gd = FNET_GROUP_DIM
    groups = fw // gd
    cc, sc = _dft_cos_sin(gd)
    fc = _mxu_const(np.concatenate([cc, -sc], axis=1))
    cl, sl = _dft_cos_sin(l_)
    cs = _mxu_const(np.concatenate([cl, sl], axis=1))
    norm = 1.0 / math.sqrt(l_ * gd)
    const2 = lambda b, g: (0, 0)
    return pl.pallas_call(
        functools.partial(_fnet_ctx_kernel, norm=norm),
        grid=(b_, groups),
        in_specs=[
            pl.BlockSpec((None, l_, gd), lambda b, g: (b, 0, g)),
            pl.BlockSpec(fc.shape, const2),
            pl.BlockSpec(cs.shape, const2),
            pl.BlockSpec((None, gd, gd), lambda b, g: (g, 0, 0)),
            pl.BlockSpec((None, 1, gd), lambda b, g: (g, 0, 0)),
        ],
        out_specs=pl.BlockSpec((None, l_, gd), lambda b, g: (b, 0, g)),
        out_shape=jax.ShapeDtypeStruct((b_, l_, fw), F32),
        compiler_params=_cparams(("parallel", "parallel")),
        name="fnet_ctx",
    )(zf, fc, cs, fnet_w_bf, fnet_b.reshape(groups, 1, gd))


def _deepnorm(resid, gate, y, g, b):
    v = DEEPNORM_ALPHA * resid + gate * y
    mu = jnp.mean(v, axis=-1, keepdims=True)
    vc = v - mu
    var = jnp.mean(vc * vc, axis=-1, keepdims=True)
    return vc * lax.rsqrt(var + LN_EPS) * g + b


def _out0_kernel(hf_ref, hb_ref, f_ref, sg_ref, x_ref, mod_ref, w_ref, lg_ref, lb_ref, o_ref, *, lru_w):
    r = hf_ref[...].astype(F32) + hb_ref[...].astype(F32)
    m_l = (r * sg_ref[:, :lru_w].astype(F32)).astype(BF16)
    m_f = (f_ref[...] * sg_ref[:, lru_w:].astype(F32)).astype(BF16)
    y = jnp.dot(m_l, w_ref[:lru_w, :], preferred_element_type=F32)
    y = y + jnp.dot(m_f, w_ref[lru_w:, :], preferred_element_type=F32)
    o_ref[...] = _deepnorm(x_ref[...], mod_ref[2:3, :], y, lg_ref[...], lb_ref[...])


def _out0(hf, hb, f, sg, x, mod, w_bf, ln_g, ln_b, tm):
    b_, l_, d = x.shape
    lru_w = hf.shape[-1]
    fn_w = f.shape[-1]
    mix = lru_w + fn_w
    row = lambda b, i: (b, i, 0)
    const2 = lambda b, i: (0, 0)
    return pl.pallas_call(
        functools.partial(_out0_kernel, lru_w=lru_w),
        grid=(b_, l_ // tm),
        in_specs=[
            pl.BlockSpec((None, tm, lru_w), row),
            pl.BlockSpec((None, tm, lru_w), row),
            pl.BlockSpec((None, tm, fn_w), row),
            pl.BlockSpec((None, tm, mix), row),
            pl.BlockSpec((None, tm, d), row),
            pl.BlockSpec((None, 3, d), lambda b, i: (b, 0, 0)),
            pl.BlockSpec((mix, d), const2),
            pl.BlockSpec((1, d), const2),
            pl.BlockSpec((1, d), const2),
        ],
        out_specs=pl.BlockSpec((None, tm, d), row),
        out_shape=jax.ShapeDtypeStruct((b_, l_, d), F32),
        compiler_params=_cparams(("parallel", "parallel")),
        name="out_proj0",
    )(hf, hb, f, sg, x, mod, w_bf, ln_g, ln_b)


def _out1_kernel(o_ref_in, sg_ref, x_ref, mod_ref, w_ref, lg_ref, lb_ref, o_ref):
    m = (o_ref_in[...].astype(F32) * sg_ref[...].astype(F32)).astype(BF16)
    y = jnp.dot(m, w_ref[...], preferred_element_type=F32)
    o_ref[...] = _deepnorm(x_ref[...], mod_ref[2:3, :], y, lg_ref[...], lb_ref[...])


def _out1(o, sg, x, mod, w_bf, ln_g, ln_b, tm):
    b_, l_, d = x.shape
    wdt = o.shape[-1]
    row = lambda b, i: (b, i, 0)
    const2 = lambda b, i: (0, 0)
    return pl.pallas_call(
        _out1_kernel,
        grid=(b_, l_ // tm),
        in_specs=[
            pl.BlockSpec((None, tm, wdt), row),
            pl.BlockSpec((None, tm, wdt), row),
            pl.BlockSpec((None, tm, d), row),
            pl.BlockSpec((None, 3, d), lambda b, i: (b, 0, 0)),
            pl.BlockSpec((wdt, d), const2),
            pl.BlockSpec((1, d), const2),
            pl.BlockSpec((1, d), const2),
        ],
        out_specs=pl.BlockSpec((None, tm, d), row),
        out_shape=jax.ShapeDtypeStruct((b_, l_, d), F32),
        compiler_params=_cparams(("parallel", "parallel")),
        name="out_proj1",
    )(o, sg, x, mod, w_bf, ln_g, ln_b)


def _rms(x, g):
    return x * lax.rsqrt(jnp.mean(x * x, axis=-1, keepdims=True) + RMS_EPS) * g


def _in1_kernel(*refs, rope, want_q):
    if want_q:
        (x_ref, mod_ref, w_ref, qg_ref, kg_ref, wq_ref, wqs_ref, wk_ref, wvt_ref, cos_ref, sin_ref,
         q_ref, k_ref, vt_ref, sg_ref) = refs
    else:
        (x_ref, mod_ref, w_ref, kg_ref, wk_ref, wvt_ref, k_ref, vt_ref) = refs
    shift = mod_ref[0:1, :]
    scale = mod_ref[1:2, :]
    u = (x_ref[...] * (1.0 + scale) + shift).astype(BF16)
    hp = HEAD_PAD
    o_kv = Q_LORA
    o_kr = Q_LORA + KV_LORA
    o_g = o_kr + 2 * hp

    kvn = _rms(jnp.dot(u, w_ref[:, o_kv:o_kv + KV_LORA], preferred_element_type=F32), kg_ref[...]).astype(BF16)
    kr = jnp.dot(u, w_ref[:, o_kr:o_kr + 2 * hp], preferred_element_type=F32)
    if rope:
        cos = cos_ref[...]
        sin = sin_ref[...]
        krope = kr[:, :hp] * cos + kr[:, hp:] * sin
    else:
        krope = kr[:, :hp]
    for h2 in range(MLA_HEADS // 2):
        cs = slice(2 * h2 * hp, (2 * h2 + 2) * hp)
        kn = jnp.dot(kvn, wk_ref[:, cs], preferred_element_type=F32)
        k_ref[2 * h2] = (kn[:, :hp] + krope).astype(BF16)
        k_ref[2 * h2 + 1] = (kn[:, hp:] + krope).astype(BF16)
    vt = lax.dot_general(wvt_ref[...], kvn, (((1,), (1,)), ((), ())), preferred_element_type=F32)
    row = lax.broadcasted_iota(jnp.int32, (vt.shape[0], 1), 0)
    vt_ref[...] = (vt + (row % VT_ROWS == V_DIM).astype(F32)).astype(BF16)

    if want_q:
        qn = _rms(jnp.dot(u, w_ref[:, :Q_LORA], preferred_element_type=F32), qg_ref[...]).astype(BF16)
        qscale = ATTN_SCALE * LOG2E
        cq = cos * qscale
        sq = sin * qscale
        for h2 in range(MLA_HEADS // 2):
            cs = slice(2 * h2 * hp, (2 * h2 + 2) * hp)
            z1 = jnp.dot(qn, wq_ref[:, cs], preferred_element_type=F32)
            z2 = jnp.dot(qn, wqs_ref[:, cs], preferred_element_type=F32)
            q_ref[2 * h2] = (z1[:, :hp] * cq + z2[:, :hp] * sq).astype(BF16)
            q_ref[2 * h2 + 1] = (z1[:, hp:] * cq + z2[:, hp:] * sq).astype(BF16)
        gw = sg_ref.shape[-1]
        for c0 in range(0, gw, 512):
            g = jnp.dot(u, w_ref[:, o_g + c0:o_g + c0 + 512], preferred_element_type=F32)
            sg_ref[:, c0:c0 + 512] = _silu(g).astype(BF16)


def _in1(x, mod, w_in_p, qg, kg, wq_p, wqs_p, wk_p, wvt_p, cos_t, sin_t, tm, *, want_q):
    b_, l_, d = x.shape
    hp = HEAD_PAD
    nh = MLA_HEADS
    gw = nh * V_DIM
    row = lambda b, i: (b, i, 0)
    const2 = lambda b, i: (0, 0)
    head = lambda b, i: (b, 0, i, 0)
    kv_specs = [pl.BlockSpec((None, nh, tm, hp), head),
                pl.BlockSpec((None, nh * VT_ROWS, tm), lambda b, i: (b, 0, i))]
    kv_shapes = [jax.ShapeDtypeStruct((b_, nh, l_, hp), BF16),
                 jax.ShapeDtypeStruct((b_, nh * VT_ROWS, l_), BF16)]
    if want_q:
        ins = [x, mod, w_in_p, qg, kg, wq_p, wqs_p, wk_p, wvt_p, cos_t, sin_t]
        in_specs = [
            pl.BlockSpec((None, tm, d), row),
            pl.BlockSpec((None, 3, d), lambda b, i: (b, 0, 0)),
            pl.BlockSpec(w_in_p.shape, const2),
            pl.BlockSpec(qg.shape, const2),
            pl.BlockSpec(kg.shape, const2),
            pl.BlockSpec(wq_p.shape, const2),
            pl.BlockSpec(wqs_p.shape, const2),
            pl.BlockSpec(wk_p.shape, const2),
            pl.BlockSpec(wvt_p.shape, const2),
            pl.BlockSpec((tm, hp), lambda b, i: (i, 0)),
            pl.BlockSpec((tm, hp), lambda b, i: (i, 0)),
        ]
        out_specs = [pl.BlockSpec((None, nh, tm, hp), head)] + kv_specs + [pl.BlockSpec((None, tm, gw), row)]
        out_shape = [jax.ShapeDtypeStruct((b_, nh, l_, hp), BF16)] + kv_shapes + [
            jax.ShapeDtypeStruct((b_, l_, gw), BF16)]
    else:
        ins = [x, mod, w_in_p, kg, wk_p, wvt_p]
        in_specs = [
            pl.BlockSpec((None, tm, d), row),
            pl.BlockSpec((None, 3, d), lambda b, i: (b, 0, 0)),
            pl.BlockSpec(w_in_p.shape, const2),
            pl.BlockSpec(kg.shape, const2),
            pl.BlockSpec(wk_p.shape, const2),
            pl.BlockSpec(wvt_p.shape, const2),
        ]
        out_specs = kv_specs
        out_shape = kv_shapes
    return pl.pallas_call(
        functools.partial(_in1_kernel, rope=want_q, want_q=want_q),
        grid=(b_, l_ // tm),
        in_specs=in_specs,
        out_specs=out_specs,
        out_shape=out_shape,
        compiler_params=_cparams(("parallel", "parallel")),
        name="in_proj1" if want_q else "in_proj1_ctx",
    )(*ins)


def _attn_kernel(q_ref, kc_ref, vtc_ref, k_ref, vt_ref, o_ref, m_sc, acc_sc, s_sc, mx_sc):
    sub = ATTN_SUB
    n_sub = k_ref.shape[1] // sub

    m_sc[...] = jnp.full(m_sc.shape, NEG_BIG, F32)
    acc_sc[...] = jnp.zeros(acc_sc.shape, F32)

    def scores(slot, k_fn):
        for hh in range(2):
            s = lax.dot_general(k_fn(hh), q_ref[hh], (((1,), (1,)), ((), ())), preferred_element_type=F32)
            s_sc[slot, hh] = s
            mx_sc[slot, hh] = jnp.max(s, axis=0, keepdims=True)

    def update(hh, s, s_max, vt):
        m_prev = m_sc[hh]
        m_new = jnp.maximum(m_prev, s_max)
        alpha = jnp.exp2(m_prev - m_new)
        p = jnp.exp2(s - m_new).astype(BF16)
        acc_sc[hh] = alpha * acc_sc[hh] + jnp.dot(vt, p, preferred_element_type=F32)
        m_sc[hh] = m_new

    def softmax_pv(slot, vt_fn):
        for hh in range(2):
            update(hh, s_sc[slot, hh], mx_sc[slot, hh], vt_fn(hh))

    def k_at(j):
        o = pl.multiple_of(j * sub, sub)
        return lambda hh: k_ref[hh, pl.ds(o, sub), :]

    def vt_at(j):
        o = pl.multiple_of(j * sub, sub)
        return lambda hh: vt_ref[hh * VT_ROWS:(hh + 1) * VT_ROWS, pl.ds(o, sub)]

    scores(0, k_at(0))
    for hh in range(2):
        s = lax.dot_general(kc_ref[hh], q_ref[hh], (((1,), (1,)), ((), ())), preferred_element_type=F32)
        update(hh, s, jnp.max(s, axis=0, keepdims=True), vtc_ref[hh * VT_ROWS:(hh + 1) * VT_ROWS, :])

    def body(i, carry):
        j = 2 * i
        scores(1, k_at(j + 1))
        softmax_pv(0, vt_at(j))
        scores(0, k_at(jnp.minimum(j + 2, n_sub - 1)))
        softmax_pv(1, vt_at(j + 1))
        return carry

    lax.fori_loop(0, n_sub // 2, body, 0, unroll=math.gcd(ATTN_UNROLL, n_sub // 2))

    halves = []
    for hh in range(2):
        acc = acc_sc[hh]
        halves.append(acc[:V_DIM] * (1.0 / acc[V_DIM:V_DIM + 1]))
    o_ref[...] = jnp.concatenate(halves, axis=0).T.astype(BF16)


def _attention(q, k_ctx, vt_ctx, k_lat, vt_lat, tq):
    b_, nh, s_, hp = q.shape
    c_ = k_ctx.shape[2]
    assert s_ % (2 * ATTN_SUB) == 0
    return pl.pallas_call(
        _attn_kernel,
        grid=(b_, nh // 2, s_ // tq),
        in_specs=[
            pl.BlockSpec((None, 2, tq, hp), lambda b, h, i: (b, h, i, 0)),
            pl.BlockSpec((None, 2, c_, hp), lambda b, h, i: (b, h, 0, 0)),
            pl.BlockSpec((None, 2 * VT_ROWS, c_), lambda b, h, i: (b, h, 0)),
            pl.BlockSpec((None, 2, s_, hp), lambda b, h, i: (b, h, 0, 0)),
            pl.BlockSpec((None, 2 * VT_ROWS, s_), lambda b, h, i: (b, h, 0)),
        ],
        out_specs=pl.BlockSpec((None, tq, 2 * V_DIM), lambda b, h, i: (b, i, h)),
        out_shape=jax.ShapeDtypeStruct((b_, s_, nh * V_DIM), BF16),
        scratch_shapes=[pltpu.VMEM((2, 1, tq), F32), pltpu.VMEM((2, VT_ROWS, tq), F32),
                        pltpu.VMEM((2, 2, ATTN_SUB, tq), F32), pltpu.VMEM((2, 2, 1, tq), F32)],
        compiler_params=_cparams(("parallel", "parallel", "arbitrary")),
        name="mla_attention",
    )(q, k_ctx, vt_ctx, k_lat, vt_lat)


def _gate_blocks(gate_w_d):
    hd = LRU_HEAD_DIM
    per = LRU_GROUP // hd
    ng = LRU_HEADS // per
    w = gate_w_d.reshape(2, ng, per, hd, hd)
    eye = jnp.eye(per, dtype=gate_w_d.dtype)
    blk = jnp.einsum('gnpij,pq->gnpiqj', w, eye).reshape(2, ng, LRU_GROUP, LRU_GROUP)
    return (0.5 * jnp.concatenate([blk[0], blk[1]], axis=-1)).astype(BF16)


def _pad_heads(w, per_head, offset=0):
    k = w.shape[0]
    w3 = w.reshape(k, MLA_HEADS, per_head)
    out = jnp.zeros((k, MLA_HEADS, HEAD_PAD), w.dtype)
    out = out.at[:, :, offset:offset + per_head].set(w3)
    return out.reshape(k, MLA_HEADS * HEAD_PAD)


def _rope_swap_idx():
    return np.arange(QK_ROPE) ^ ROPE_PAIRS


def _prep_mla(w_in, w_uq, w_ukv):
    d = w_in.shape[0]
    qc = w_in[:, :Q_LORA]
    kvc = w_in[:, Q_LORA:Q_LORA + KV_LORA]
    kr = w_in[:, Q_LORA + KV_LORA:Q_LORA + KV_LORA + QK_ROPE]
    g = w_in[:, Q_LORA + KV_LORA + QK_ROPE:]
    swap = _rope_swap_idx()
    kr_blk = jnp.zeros((d, HEAD_PAD), w_in.dtype).at[:, QK_NOPE:QK_NOPE + QK_ROPE].set(kr)
    krs_blk = jnp.zeros((d, HEAD_PAD), w_in.dtype).at[:, QK_NOPE:QK_NOPE + QK_ROPE].set(kr[:, swap])
    w_in_p = jnp.concatenate([qc, kvc, kr_blk, krs_blk, g], axis=1).astype(BF16)

    wq3 = w_uq.reshape(Q_LORA, MLA_HEADS, QK_DIM)
    wq_p = _pad_heads(w_uq, QK_DIM).astype(BF16)
    rope_sw = wq3[:, :, QK_NOPE:][:, :, swap]
    wqs = jnp.zeros((Q_LORA, MLA_HEADS, HEAD_PAD), w_uq.dtype).at[:, :, QK_NOPE:QK_NOPE + QK_ROPE].set(rope_sw)
    wqs_p = wqs.reshape(Q_LORA, MLA_HEADS * HEAD_PAD).astype(BF16)

    wkv3 = w_ukv.reshape(KV_LORA, MLA_HEADS, QK_NOPE + V_DIM)
    wk_p = _pad_heads(wkv3[:, :, :QK_NOPE].reshape(KV_LORA, -1), QK_NOPE).astype(BF16)
    wv3 = jnp.transpose(wkv3[:, :, QK_NOPE:], (1, 2, 0))
    wvt_p = jnp.pad(wv3, ((0, 0), (0, VT_ROWS - V_DIM), (0, 0))).reshape(MLA_HEADS * VT_ROWS, KV_LORA)
    return w_in_p, wq_p, wqs_p, wk_p, wvt_p.astype(BF16)


def _rope_tables(s_):
    rows = s_ // GRID_W
    inv = ROPE_THETA ** (-jnp.arange(ROPE_PAIRS, dtype=F32) / ROPE_PAIRS)
    a_r = jnp.arange(rows, dtype=F32)[:, None] * inv
    a_c = jnp.arange(GRID_W, dtype=F32)[:, None] * inv
    shape = (rows, GRID_W, ROPE_PAIRS)
    by_row = lambda t: jnp.broadcast_to(t[:, None, :], shape)
    by_col = lambda t: jnp.broadcast_to(t[None, :, :], shape)
    cr, sr, cc, sc = by_row(jnp.cos(a_r)), by_row(jnp.sin(a_r)), by_col(jnp.cos(a_c)), by_col(jnp.sin(a_c))
    pad = HEAD_PAD - QK_NOPE - QK_ROPE
    cos_t = jnp.concatenate([jnp.ones(shape[:2] + (QK_NOPE,), F32), cr, cr, cc, cc,
                             jnp.ones(shape[:2] + (pad,), F32)], axis=-1)
    sin_t = jnp.concatenate([jnp.zeros(shape[:2] + (QK_NOPE,), F32), -sr, sr, -sc, sc,
                             jnp.zeros(shape[:2] + (pad,), F32)], axis=-1)
    return cos_t.reshape(s_, HEAD_PAD), sin_t.reshape(s_, HEAD_PAD)


def kernel(x, c, ctx, c_ctx, ada_w, ada_b, ln_g, ln_b, w_in_rf, conv_w, conv_b, lru_gate_w, lru_gate_b,
           lru_lambda, fnet_w, fnet_b, w_out_rf, w_in_mla, q_norm_g, kv_norm_g, w_uq, w_ukv, w_out_mla):
    b_, s_, d = x.shape
    c_len = ctx.shape[1]
    tm = 512 if s_ % 512 == 0 else s_
    t_scan = 512 if s_ % 512 == 0 else s_

    cond = jnp.zeros((SUBLANES, d), F32).at[:b_].set(c).at[b_].set(c_ctx)
    mod = _adaln(cond, ada_w, ada_b).reshape(DEPTH, SUBLANES, 3, d)
    mod_lat = [mod[l, :b_] for l in range(DEPTH)]
    mod_ctx = [jnp.broadcast_to(mod[l, b_][None], (b_, 3, d)) for l in range(DEPTH)]

    w_in0 = w_in_rf[0].astype(BF16)
    w_out0 = w_out_rf[0].astype(BF16)
    fw_bf = fnet_w[0].astype(BF16)
    wg = [_gate_blocks(lru_gate_w[0, dd]) for dd in range(2)]
    cb = conv_b[0].reshape(1, -1)

    zl_c, zf_c, sg_c = _in0(ctx, mod_ctx[0], w_in0, c_len)
    zl_l, zf_l, sg_l = _in0(x, mod_lat[0], w_in0, tm)

    h_c, h_l = [], []
    for dd in range(2):
        rev = dd == 1
        lam = lru_lambda[0, dd].reshape(1, -1)
        zero = jnp.zeros((b_, 1, d), F32)
        hc, hc_fin = _lru_scan(zl_c, conv_w[0], cb, wg[dd], lru_gate_b[0, dd], lam, zero,
                               reverse=rev, t_rows=c_len)
        hl, _ = _lru_scan(zl_l, conv_w[0], cb, wg[dd], lru_gate_b[0, dd], lam, hc_fin,
                          reverse=rev, t_rows=t_scan)
        h_c.append(hc)
        h_l.append(hl)

    f_c = _fnet_ctx(zf_c, fw_bf, fnet_b[0])
    f_l = _fnet_lat(zf_l, fw_bf, fnet_b[0])

    lg0, lb0 = ln_g[0].reshape(1, d), ln_b[0].reshape(1, d)
    h1_c = _out0(h_c[0], h_c[1], f_c, sg_c, ctx, mod_ctx[0], w_out0, lg0, lb0, c_len)
    h1_l = _out0(h_l[0], h_l[1], f_l, sg_l, x, mod_lat[0], w_out0, lg0, lb0, tm)

    w_in_p, wq_p, wqs_p, wk_p, wvt_p = _prep_mla(w_in_mla[0], w_uq[0], w_ukv[0])
    qg = q_norm_g[0].reshape(1, -1)
    kg = kv_norm_g[0].reshape(1, -1)
    cos_t, sin_t = _rope_tables(s_)

    k_c, v_c = _in1(h1_c, mod_ctx[1], w_in_p, qg, kg, wq_p, wqs_p, wk_p, wvt_p, cos_t, sin_t, c_len,
                    want_q=False)
    q_l, k_l, v_l, sg1 = _in1(h1_l, mod_lat[1], w_in_p, qg, kg, wq_p, wqs_p, wk_p, wvt_p, cos_t, sin_t, tm,
                              want_q=True)

    tq = 512 if s_ % 1024 == 0 else s_
    o = _attention(q_l, k_c, v_c, k_l, v_l, tq)

    lg1, lb1 = ln_g[1].reshape(1, d), ln_b[1].reshape(1, d)
    return _out1(o, sg1, h1_l, mod_lat[1], w_out_mla[0].astype(BF16), lg1, lb1, tm)
```

```python
import functools
import math

import numpy as np
import jax
import jax.numpy as jnp
from jax import lax
from jax.experimental import pallas as pl
from jax.experimental.pallas import tpu as pltpu

F32 = jnp.float32
BF16 = jnp.bfloat16

DEPTH = 2
GRID_W = 64
DEEPNORM_ALPHA = (2 * DEPTH) ** 0.25
LN_EPS = 1e-6
RMS_EPS = 1e-6
LRU_HEADS = 16
LRU_HEAD_DIM = 64
LRU_GROUP = 256
CONV_W = 4
LRU_C = 8.0
FNET_GROUPS = 4
FNET_GROUP_DIM = 128
MLA_HEADS = 16
Q_LORA = 256
KV_LORA = 128
QK_NOPE = 64
QK_ROPE = 32
V_DIM = 64
QK_DIM = QK_NOPE + QK_ROPE
ROPE_PAIRS = QK_ROPE // 4
ROPE_THETA = 10000.0
ATTN_SCALE = QK_DIM ** -0.5
LOG2E = 1.4426950408889634

LANES = 128
SUBLANES = 8
BF16_ROWS = 16
VMEM_LIMIT = 56 * 1024 * 1024

HEAD_PAD = LANES
VT_ROWS = V_DIM + BF16_ROWS
ATTN_SUB = 256
ATTN_UNROLL = 16
FFT_N2 = 128
FFT_PITCH = FFT_N2 + SUBLANES
FFT_BATCH = 16
NEG_BIG = -1e30


def _sigmoid(x):
    return 0.5 * (jnp.tanh(0.5 * x) + 1.0)


def _silu(x):
    return x * _sigmoid(x)


def _cparams(sem):
    return pltpu.CompilerParams(dimension_semantics=sem, vmem_limit_bytes=VMEM_LIMIT)


def _adaln_kernel(cond_ref, w_ref, b_ref, o_ref):
    c = cond_ref[...]
    o_ref[...] = jnp.dot(_silu(c), w_ref[...], preferred_element_type=F32,
                         precision=lax.Precision.HIGHEST) + b_ref[...]


def _adaln(cond, ada_w, ada_b):
    depth, d, n = ada_w.shape
    rows = cond.shape[0]
    tn = 768
    return pl.pallas_call(
        _adaln_kernel,
        grid=(depth, n // tn),
        in_specs=[
            pl.BlockSpec((rows, d), lambda l, j: (0, 0)),
            pl.BlockSpec((None, d, tn), lambda l, j: (l, 0, j)),
            pl.BlockSpec((None, 1, tn), lambda l, j: (l, 0, j)),
        ],
        out_specs=pl.BlockSpec((None, rows, tn), lambda l, j: (l, 0, j)),
        out_shape=jax.ShapeDtypeStruct((depth, rows, n), F32),
        compiler_params=_cparams(("parallel", "parallel")),
        name="adaln",
    )(cond, ada_w, ada_b.reshape(depth, 1, n))


def _in0_kernel(x_ref, mod_ref, w_ref, zl_ref, zf_ref, sg_ref, *, lru_w, fn_w):
    shift = mod_ref[0:1, :]
    scale = mod_ref[1:2, :]
    u = (x_ref[...] * (1.0 + scale) + shift).astype(BF16)
    zl_ref[...] = jnp.dot(u, w_ref[:, :lru_w], preferred_element_type=F32).astype(BF16)
    zf_ref[...] = jnp.dot(u, w_ref[:, lru_w:lru_w + fn_w], preferred_element_type=F32).astype(BF16)
    g0 = lru_w + fn_w
    for c0 in range(0, lru_w + fn_w, 512):
        g = jnp.dot(u, w_ref[:, g0 + c0:g0 + c0 + 512], preferred_element_type=F32)
        sg_ref[:, c0:c0 + 512] = _silu(g).astype(BF16)


def _in0(x, mod, w_bf, tm):
    b_, l_, d = x.shape
    lru_w, fn_w = d, d // 2
    mix = lru_w + fn_w
    return pl.pallas_call(
        functools.partial(_in0_kernel, lru_w=lru_w, fn_w=fn_w),
        grid=(b_, l_ // tm),
        in_specs=[
            pl.BlockSpec((None, tm, d), lambda b, i: (b, i, 0)),
            pl.BlockSpec((None, 3, d), lambda b, i: (b, 0, 0)),
            pl.BlockSpec((d, 2 * mix), lambda b, i: (0, 0)),
        ],
        out_specs=[
            pl.BlockSpec((None, tm, lru_w), lambda b, i: (b, i, 0)),
            pl.BlockSpec((None, tm, fn_w), lambda b, i: (b, i, 0)),
            pl.BlockSpec((None, tm, mix), lambda b, i: (b, i, 0)),
        ],
        out_shape=[
            jax.ShapeDtypeStruct((b_, l_, lru_w), BF16),
            jax.ShapeDtypeStruct((b_, l_, fn_w), BF16),
            jax.ShapeDtypeStruct((b_, l_, mix), BF16),
        ],
        compiler_params=_cparams(("parallel", "parallel")),
        name="in_proj0",
    )(x, mod, w_bf)


def _scan_kernel(z_ref, zp_ref, zn_ref, cw_ref, cb_ref, wg_ref, gb_ref, lam_ref, h0_ref,
                 h_ref, hfin_ref, zz_sc, a_sc, b_sc, hs_sc, carry_sc, *, reverse, t_rows):
    i = pl.program_id(0)
    nt = pl.num_programs(0)
    tt = (nt - 1 - i) if reverse else i
    halo = BF16_ROWS
    nb = z_ref.shape[0]
    width = z_ref.shape[-1]

    @pl.when(i == 0)
    def _():
        carry_sc[...] = jnp.broadcast_to(h0_ref[...], carry_sc.shape)

    nl = -lam_ref[...]
    half_c = (0.5 * LRU_C) * (jnp.maximum(nl, 0.0) + jnp.log(1.0 + jnp.exp(-jnp.abs(nl))))
    half_gb = 0.5 * gb_ref[...]
    n_buf = t_rows + 2 * halo
    for bi in range(nb):
        zz_sc[0:halo, :] = jnp.where(tt > 0, zp_ref[bi].astype(F32), 0.0)
        zz_sc[halo:halo + t_rows, :] = z_ref[bi].astype(F32)
        zz_sc[halo + t_rows:halo + t_rows + halo, :] = jnp.where(tt < nt - 1, zn_ref[bi].astype(F32), 0.0)
        for j in range(width // LRU_GROUP):
            cs = slice(j * LRU_GROUP, (j + 1) * LRU_GROUP)
            zf = zz_sc[:, cs]
            xc = cb_ref[:, cs] + cw_ref[1:2, cs] * zf[halo:halo + t_rows]
            for k, shift in ((0, 1), (2, n_buf - 1), (3, n_buf - 2)):
                xc = xc + cw_ref[k:k + 1, cs] * pltpu.roll(zf, shift, axis=0)[halo:halo + t_rows]
            g = jnp.dot(xc.astype(BF16), wg_ref[j], preferred_element_type=F32)
            t_r = jnp.tanh(g[:, :LRU_GROUP] + half_gb[0:1, cs])
            t_i = jnp.tanh(g[:, LRU_GROUP:] + half_gb[1:2, cs])
            hc = half_c[:, cs]
            nla = hc * t_r + hc
            a = jnp.exp2(nla * (-LOG2E))
            y = jnp.tanh(nla) * (1.0 + a * a)
            sq = jnp.where(y > 0.0, y * lax.rsqrt(y), 0.0)
            hx = 0.5 * xc
            a_sc[bi, :, cs] = a
            b_sc[bi, :, cs] = sq * (hx * t_i + hx)

    def group(gi, hs):
        base = (t_rows - SUBLANES - gi * SUBLANES) if reverse else gi * SUBLANES
        base = pl.multiple_of(base, SUBLANES)
        hs = list(hs)
        for t in range(SUBLANES):
            row = base + ((SUBLANES - 1 - t) if reverse else t)
            for bi in range(nb):
                hs[bi] = a_sc[bi, pl.ds(row, 1), :] * hs[bi] + b_sc[bi, pl.ds(row, 1), :]
                hs_sc[bi, pl.ds(row, 1), :] = hs[bi]
        return tuple(hs)

    h_last = lax.fori_loop(0, t_rows // SUBLANES, group, tuple(carry_sc[bi, 0:1, :] for bi in range(nb)))
    for bi in range(nb):
        carry_sc[bi, 0:1, :] = h_last[bi]
    h_ref[...] = hs_sc[...].astype(BF16)

    @pl.when(i == nt - 1)
    def _():
        for bi in range(nb):
            hfin_ref[bi] = h_last[bi]


def _lru_scan(zl, conv_w, conv_b, wg_d, gb_d, lam_d, h0_d, *, reverse, t_rows):
    b_, l_, w = zl.shape
    nt = l_ // t_rows
    hb = t_rows // BF16_ROWS
    n_hb = l_ // BF16_ROWS

    def tile(i):
        return (nt - 1 - i) if reverse else i

    return pl.pallas_call(
        functools.partial(_scan_kernel, reverse=reverse, t_rows=t_rows),
        grid=(nt,),
        in_specs=[
            pl.BlockSpec((b_, t_rows, w), lambda i: (0, tile(i), 0)),
            pl.BlockSpec((b_, BF16_ROWS, w), lambda i: (0, jnp.maximum(tile(i) * hb - 1, 0), 0)),
            pl.BlockSpec((b_, BF16_ROWS, w), lambda i: (0, jnp.minimum((tile(i) + 1) * hb, n_hb - 1), 0)),
            pl.BlockSpec((CONV_W, w), lambda i: (0, 0)),
            pl.BlockSpec((1, w), lambda i: (0, 0)),
            pl.BlockSpec((w // LRU_GROUP, LRU_GROUP, 2 * LRU_GROUP), lambda i: (0, 0, 0)),
            pl.BlockSpec((2, w), lambda i: (0, 0)),
            pl.BlockSpec((1, w), lambda i: (0, 0)),
            pl.BlockSpec((b_, 1, w), lambda i: (0, 0, 0)),
        ],
        out_specs=[
            pl.BlockSpec((b_, t_rows, w), lambda i: (0, tile(i), 0)),
            pl.BlockSpec((b_, 1, w), lambda i: (0, 0, 0)),
        ],
        out_shape=[
            jax.ShapeDtypeStruct((b_, l_, w), BF16),
            jax.ShapeDtypeStruct((b_, 1, w), F32),
        ],
        scratch_shapes=[
            pltpu.VMEM((t_rows + 2 * BF16_ROWS, w), F32),
            pltpu.VMEM((b_, t_rows, w), F32),
            pltpu.VMEM((b_, t_rows, w), F32),
            pltpu.VMEM((b_, t_rows, w), F32),
            pltpu.VMEM((b_, SUBLANES, w), F32),
        ],
        compiler_params=_cparams(("arbitrary",)),
        name="lru_scan_bwd" if reverse else "lru_scan_fwd",
    )(zl, zl, zl, conv_w, conv_b, wg_d, gb_d, lam_d, h0_d)


def _dft_cos_sin(n):
    k = np.arange(n, dtype=np.int64)
    ang = 2.0 * np.pi * ((k[:, None] * k[None, :]) % n).astype(np.float64) / n
    return np.cos(ang), np.sin(ang)


def _mxu_const(a):
    return jnp.asarray(a, F32).astype(BF16)


def _fnet_lat_kernel(x_ref, fc_ref, m1_ref, c2s2_ref, twa_ref, twb_ref, fw_ref, fb_ref, o_ref, s_sc,
                     *, n1, norm):
    n2 = FFT_N2
    pitch = FFT_PITCH
    gd = x_ref.shape[-1]

    nb = FFT_BATCH

    def stage0(i, carry):
        r0 = pl.multiple_of(i * (nb * n2), nb * n2)
        u = jnp.dot(x_ref[pl.ds(r0, nb * n2), :], fc_ref[...], preferred_element_type=F32)
        for j in range(nb):
            d0 = pl.multiple_of((i * nb + j) * pitch, SUBLANES)
            s_sc[0, pl.ds(d0, n2), :] = u[j * n2:(j + 1) * n2, :gd]
            s_sc[1, pl.ds(d0, n2), :] = u[j * n2:(j + 1) * n2, gd:]
        return carry

    lax.fori_loop(0, n1 // nb, stage0, 0)

    def stage1(a, carry):
        ta_r = twa_ref[0, a]
        ta_i = twa_ref[1, a]
        gs = []
        for b in range(SUBLANES):
            col = a * SUBLANES + b
            g_r = s_sc[0, pl.ds(col, n1, stride=pitch), :]
            g_i = s_sc[1, pl.ds(col, n1, stride=pitch), :]
            gs.append(jnp.concatenate([g_r, g_i], axis=0).astype(BF16))
        y_all = jnp.dot(m1_ref[...], jnp.concatenate(gs, axis=1), preferred_element_type=F32)
        for b in range(SUBLANES):
            col = a * SUBLANES + b
            y_r = y_all[:n1, b * gd:(b + 1) * gd]
            y_i = y_all[n1:, b * gd:(b + 1) * gd]
            tb_r = twb_ref[0, b]
            tb_i = twb_ref[1, b]
            c = ta_r * tb_r - ta_i * tb_i
            s = -(ta_r * tb_i + ta_i * tb_r)
            s_sc[0, pl.ds(col, n1, stride=pitch), :] = y_r * c + y_i * s
            s_sc[1, pl.ds(col, n1, stride=pitch), :] = y_i * c - y_r * s
        return carry

    lax.fori_loop(0, n2 // SUBLANES, stage1, 0)

    def stage2(i, carry):
        zs = []
        for j in range(nb):
            d0 = pl.multiple_of((i * nb + j) * pitch, SUBLANES)
            zs.append(jnp.concatenate([s_sc[0, pl.ds(d0, n2), :], s_sc[1, pl.ds(d0, n2), :]],
                                      axis=0).astype(BF16))
        z = jnp.concatenate(zs, axis=1)
        xr = jnp.dot(c2s2_ref[...], z, preferred_element_type=F32) * norm
        xs = jnp.concatenate([xr[:, j * gd:(j + 1) * gd] for j in range(nb)], axis=0).astype(BF16)
        out = jnp.dot(xs, fw_ref[...], preferred_element_type=F32) + fb_ref[...]
        for j in range(nb):
            o_ref[pl.ds(i * nb + j, n2, stride=n1), :] = out[j * n2:(j + 1) * n2, :]
        return carry

    lax.fori_loop(0, n1 // nb, stage2, 0)


def _fnet_lat(zf, fnet_w_bf, fnet_b):
    b_, l_, fw = zf.shape
    gd = FNET_GROUP_DIM
    groups = fw // gd
    n2 = FFT_N2
    n1 = l_ // n2
    assert n1 * n2 == l_ and n1 % SUBLANES == 0
    cc, sc = _dft_cos_sin(gd)
    fc = _mxu_const(np.concatenate([cc, -sc], axis=1))
    c1, s1 = _dft_cos_sin(n1)
    m1 = _mxu_const(np.block([[c1, s1], [-s1, c1]]))
    c2, s2 = _dft_cos_sin(n2)
    c2s2 = _mxu_const(np.concatenate([c2, s2], axis=1))
    k1 = np.arange(n1, dtype=np.float64)[None, :, None]
    ang_a = 2.0 * np.pi * (np.arange(n2 // SUBLANES, dtype=np.float64) * SUBLANES)[:, None, None] * k1 / l_
    ang_b = 2.0 * np.pi * np.arange(SUBLANES, dtype=np.float64)[:, None, None] * k1 / l_
    ones = np.ones((1, 1, gd))
    twa = jnp.asarray(np.stack([np.cos(ang_a) * ones, -np.sin(ang_a) * ones]), F32)
    twb = jnp.asarray(np.stack([np.cos(ang_b) * ones, -np.sin(ang_b) * ones]), F32)
    norm = 1.0 / math.sqrt(l_ * gd)
    const2 = lambda b, g: (0, 0)
    const4 = lambda b, g: (0, 0, 0, 0)
    return pl.pallas_call(
        functools.partial(_fnet_lat_kernel, n1=n1, norm=norm),
        grid=(b_, groups),
        in_specs=[
            pl.BlockSpec((None, l_, gd), lambda b, g: (b, 0, g)),
            pl.BlockSpec(fc.shape, const2),
            pl.BlockSpec(m1.shape, const2),
            pl.BlockSpec(c2s2.shape, const2),
            pl.BlockSpec(twa.shape, const4),
            pl.BlockSpec(twb.shape, const4),
            pl.BlockSpec((None, gd, gd), lambda b, g: (g, 0, 0)),
            pl.BlockSpec((None, 1, gd), lambda b, g: (g, 0, 0)),
        ],
        out_specs=pl.BlockSpec((None, l_, gd), lambda b, g: (b, 0, g)),
        out_shape=jax.ShapeDtypeStruct((b_, l_, fw), F32),
        scratch_shapes=[pltpu.VMEM((2, n1 * FFT_PITCH, gd), F32)],
        compiler_params=_cparams(("parallel", "parallel")),
        name="fnet_lat",
    )(zf, fc, m1, c2s2, twa, twb, fnet_w_bf, fnet_b.reshape(groups, 1, gd))


def _fnet_ctx_kernel(x_ref, fc_ref, cs_ref, fw_ref, fb_ref, o_ref, *, norm):
    gd = x_ref.shape[-1]
    u = jnp.dot(x_ref[...], fc_ref[...], preferred_element_type=F32)
    z = jnp.concatenate([u[:, :gd], u[:, gd:]], axis=0).astype(BF16)
    xr = jnp.dot(cs_ref[...], z, preferred_element_type=F32) * norm
    o_ref[...] = jnp.dot(xr.astype(BF16), fw_ref[...], preferred_element_type=F32) + fb_ref[...]


def _fnet_ctx(zf, fnet_w_bf, fnet_b):
    b_, l_, fw = zf.shape
    gd = FNET_GROUP_DIM
    groups = fw // gd
    cc, sc = _dft_cos_sin(gd)
    fc = _mxu_const(np.concatenate([cc, -sc], axis=1))
    cl, sl = _dft_cos_sin(l_)
    cs = _mxu_const(np.concatenate([cl, sl], axis=1))
    norm = 1.0 / math.sqrt(l_ * gd)
    const2 = lambda b, g: (0, 0)
    return pl.pallas_call(
        functools.partial(_fnet_ctx_kernel, norm=norm),
        grid=(b_, groups),
        in_specs=[
            pl.BlockSpec((None, l_, gd), lambda b, g: (b, 0, g)),
            pl.BlockSpec(fc.shape, const2),
            pl.BlockSpec(cs.shape, const2),
            pl.BlockSpec((None, gd, gd), lambda b, g: (g, 0, 0)),
            pl.BlockSpec((None, 1, gd), lambda b, g: (g, 0, 0)),
        ],
        out_specs=pl.BlockSpec((None, l_, gd), lambda b, g: (b, 0, g)),
        out_shape=jax.ShapeDtypeStruct((b_, l_, fw), F32),
        compiler_params=_cparams(("parallel", "parallel")),
        name="fnet_ctx",
    )(zf, fc, cs, fnet_w_bf, fnet_b.reshape(groups, 1, gd))


def _deepnorm(resid, gate, y, g, b):
    v = DEEPNORM_ALPHA * resid + gate * y
    mu = jnp.mean(v, axis=-1, keepdims=True)
    vc = v - mu
    var = jnp.mean(vc * vc, axis=-1, keepdims=True)
    return vc * lax.rsqrt(var + LN_EPS) * g + b


def _out0_kernel(hf_ref, hb_ref, f_ref, sg_ref, x_ref, mod_ref, w_ref, lg_ref, lb_ref, o_ref, *, lru_w):
    r = hf_ref[...].astype(F32) + hb_ref[...].astype(F32)
    m_l = (r * sg_ref[:, :lru_w].astype(F32)).astype(BF16)
    m_f = (f_ref[...] * sg_ref[:, lru_w:].astype(F32)).astype(BF16)
    y = jnp.dot(m_l, w_ref[:lru_w, :], preferred_element_type=F32)
    y = y + jnp.dot(m_f, w_ref[lru_w:, :], preferred_element_type=F32)
    o_ref[...] = _deepnorm(x_ref[...], mod_ref[2:3, :], y, lg_ref[...], lb_ref[...])


def _out0(hf, hb, f, sg, x, mod, w_bf, ln_g, ln_b, tm):
    b_, l_, d = x.shape
    lru_w = hf.shape[-1]
    fn_w = f.shape[-1]
    mix = lru_w + fn_w
    row = lambda b, i: (b, i, 0)
    const2 = lambda b, i: (0, 0)
    return pl.pallas_call(
        functools.partial(_out0_kernel, lru_w=lru_w),
        grid=(b_, l_ // tm),
        in_specs=[
            pl.BlockSpec((None, tm, lru_w), row),
            pl.BlockSpec((None, tm, lru_w), row),
            pl.BlockSpec((None, tm, fn_w), row),
            pl.BlockSpec((None, tm, mix), row),
            pl.BlockSpec((None, tm, d), row),
            pl.BlockSpec((None, 3, d), lambda b, i: (b, 0, 0)),
            pl.BlockSpec((mix, d), const2),
            pl.BlockSpec((1, d), const2),
            pl.BlockSpec((1, d), const2),
        ],
        out_specs=pl.BlockSpec((None, tm, d), row),
        out_shape=jax.ShapeDtypeStruct((b_, l_, d), F32),
        compiler_params=_cparams(("parallel", "parallel")),
        name="out_proj0",
    )(hf, hb, f, sg, x, mod, w_bf, ln_g, ln_b)


def _out1_kernel(o_ref_in, sg_ref, x_ref, mod_ref, w_ref, lg_ref, lb_ref, o_ref):
    m = (o_ref_in[...].astype(F32) * sg_ref[...].astype(F32)).astype(BF16)
    y = jnp.dot(m, w_ref[...], preferred_element_type=F32)
    o_ref[...] = _deepnorm(x_ref[...], mod_ref[2:3, :], y, lg_ref[...], lb_ref[...])


def _out1(o, sg, x, mod, w_bf, ln_g, ln_b, tm):
    b_, l_, d = x.shape
    wdt = o.shape[-1]
    row = lambda b, i: (b, i, 0)
    const2 = lambda b, i: (0, 0)
    return pl.pallas_call(
        _out1_kernel,
        grid=(b_, l_ // tm),
        in_specs=[
            pl.BlockSpec((None, tm, wdt), row),
            pl.BlockSpec((None, tm, wdt), row),
            pl.BlockSpec((None, tm, d), row),
            pl.BlockSpec((None, 3, d), lambda b, i: (b, 0, 0)),
            pl.BlockSpec((wdt, d), const2),
            pl.BlockSpec((1, d), const2),
            pl.BlockSpec((1, d), const2),
        ],
        out_specs=pl.BlockSpec((None, tm, d), row),
        out_shape=jax.ShapeDtypeStruct((b_, l_, d), F32),
        compiler_params=_cparams(("parallel", "parallel")),
        name="out_proj1",
    )(o, sg, x, mod, w_bf, ln_g, ln_b)


def _rms(x, g):
    return x * lax.rsqrt(jnp.mean(x * x, axis=-1, keepdims=True) + RMS_EPS) * g


def _in1_kernel(*refs, rope, want_q):
    if want_q:
        (x_ref, mod_ref, w_ref, qg_ref, kg_ref, wq_ref, wqs_ref, wk_ref, wvt_ref, cos_ref, sin_ref,
         q_ref, k_ref, vt_ref, sg_ref) = refs
    else:
        (x_ref, mod_ref, w_ref, kg_ref, wk_ref, wvt_ref, k_ref, vt_ref) = refs
    shift = mod_ref[0:1, :]
    scale = mod_ref[1:2, :]
    u = (x_ref[...] * (1.0 + scale) + shift).astype(BF16)
    hp = HEAD_PAD
    o_kv = Q_LORA
    o_kr = Q_LORA + KV_LORA
    o_g = o_kr + 2 * hp

    kvn = _rms(jnp.dot(u, w_ref[:, o_kv:o_kv + KV_LORA], preferred_element_type=F32), kg_ref[...]).astype(BF16)
    kr = jnp.dot(u, w_ref[:, o_kr:o_kr + 2 * hp], preferred_element_type=F32)
    if rope:
        cos = cos_ref[...]
        sin = sin_ref[...]
        krope = kr[:, :hp] * cos + kr[:, hp:] * sin
    else:
        krope = kr[:, :hp]
    for h2 in range(MLA_HEADS // 2):
        cs = slice(2 * h2 * hp, (2 * h2 + 2) * hp)
        kn = jnp.dot(kvn, wk_ref[:, cs], preferred_element_type=F32)
        k_ref[2 * h2] = (kn[:, :hp] + krope).astype(BF16)
        k_ref[2 * h2 + 1] = (kn[:, hp:] + krope).astype(BF16)
    vt = lax.dot_general(wvt_ref[...], kvn, (((1,), (1,)), ((), ())), preferred_element_type=F32)
    row = lax.broadcasted_iota(jnp.int32, (vt.shape[0], 1), 0)
    vt_ref[...] = (vt + (row % VT_ROWS == V_DIM).astype(F32)).astype(BF16)

    if want_q:
        qn = _rms(jnp.dot(u, w_ref[:, :Q_LORA], preferred_element_type=F32), qg_ref[...]).astype(BF16)
        qscale = ATTN_SCALE * LOG2E
        cq = cos * qscale
        sq = sin * qscale
        for h2 in range(MLA_HEADS // 2):
            cs = slice(2 * h2 * hp, (2 * h2 + 2) * hp)
            z1 = jnp.dot(qn, wq_ref[:, cs], preferred_element_type=F32)
            z2 = jnp.dot(qn, wqs_ref[:, cs], preferred_element_type=F32)
            q_ref[2 * h2] = (z1[:, :hp] * cq + z2[:, :hp] * sq).astype(BF16)
            q_ref[2 * h2 + 1] = (z1[:, hp:] * cq + z2[:, hp:] * sq).astype(BF16)
        gw = sg_ref.shape[-1]
        for c0 in range(0, gw, 512):
            g = jnp.dot(u, w_ref[:, o_g + c0:o_g + c0 + 512], preferred_element_type=F32)
            sg_ref[:, c0:c0 + 512] = _silu(g).astype(BF16)


def _in1(x, mod, w_in_p, qg, kg, wq_p, wqs_p, wk_p, wvt_p, cos_t, sin_t, tm, *, want_q):
    b_, l_, d = x.shape
    hp = HEAD_PAD
    nh = MLA_HEADS
    gw = nh * V_DIM
    row = lambda b, i: (b, i, 0)
    const2 = lambda b, i: (0, 0)
    head = lambda b, i: (b, 0, i, 0)
    kv_specs = [pl.BlockSpec((None, nh, tm, hp), head),
                pl.BlockSpec((None, nh * VT_ROWS, tm), lambda b, i: (b, 0, i))]
    kv_shapes = [jax.ShapeDtypeStruct((b_, nh, l_, hp), BF16),
                 jax.ShapeDtypeStruct((b_, nh * VT_ROWS, l_), BF16)]
    if want_q:
        ins = [x, mod, w_in_p, qg, kg, wq_p, wqs_p, wk_p, wvt_p, cos_t, sin_t]
        in_specs = [
            pl.BlockSpec((None, tm, d), row),
            pl.BlockSpec((None, 3, d), lambda b, i: (b, 0, 0)),
            pl.BlockSpec(w_in_p.shape, const2),
            pl.BlockSpec(qg.shape, const2),
            pl.BlockSpec(kg.shape, const2),
            pl.BlockSpec(wq_p.shape, const2),
            pl.BlockSpec(wqs_p.shape, const2),
            pl.BlockSpec(wk_p.shape, const2),
            pl.BlockSpec(wvt_p.shape, const2),
            pl.BlockSpec((tm, hp), lambda b, i: (i, 0)),
            pl.BlockSpec((tm, hp), lambda b, i: (i, 0)),
        ]
        out_specs = [pl.BlockSpec((None, nh, tm, hp), head)] + kv_specs + [pl.BlockSpec((None, tm, gw), row)]
        out_shape = [jax.ShapeDtypeStruct((b_, nh, l_, hp), BF16)] + kv_shapes + [
            jax.ShapeDtypeStruct((b_, l_, gw), BF16)]
    else:
        ins = [x, mod, w_in_p, kg, wk_p, wvt_p]
        in_specs = [
            pl.BlockSpec((None, tm, d), row),
            pl.BlockSpec((None, 3, d), lambda b, i: (b, 0, 0)),
            pl.BlockSpec(w_in_p.shape, const2),
            pl.BlockSpec(kg.shape, const2),
            pl.BlockSpec(wk_p.shape, const2),
            pl.BlockSpec(wvt_p.shape, const2),
        ]
        out_specs = kv_specs
        out_shape = kv_shapes
    return pl.pallas_call(
        functools.partial(_in1_kernel, rope=want_q, want_q=want_q),
        grid=(b_, l_ // tm),
        in_specs=in_specs,
        out_specs=out_specs,
        out_shape=out_shape,
        compiler_params=_cparams(("parallel", "parallel")),
        name="in_proj1" if want_q else "in_proj1_ctx",
    )(*ins)


def _attn_kernel(q_ref, kc_ref, vtc_ref, k_ref, vt_ref, o_ref, m_sc, acc_sc, s_sc, mx_sc):
    sub = ATTN_SUB
    n_sub = k_ref.shape[1] // sub

    m_sc[...] = jnp.full(m_sc.shape, NEG_BIG, F32)
    acc_sc[...] = jnp.zeros(acc_sc.shape, F32)

    def scores(slot, k_fn):
        for hh in range(2):
            s = lax.dot_general(k_fn(hh), q_ref[hh], (((1,), (1,)), ((), ())), preferred_element_type=F32)
            s_sc[slot, hh] = s
            mx_sc[slot, hh] = jnp.max(s, axis=0, keepdims=True)

    def update(hh, s, s_max, vt):
        m_prev = m_sc[hh]
        m_new = jnp.maximum(m_prev, s_max)
        alpha = jnp.exp2(m_prev - m_new)
        p = jnp.exp2(s - m_new).astype(BF16)
        acc_sc[hh] = alpha * acc_sc[hh] + jnp.dot(vt, p, preferred_element_type=F32)
        m_sc[hh] = m_new

    def softmax_pv(slot, vt_fn):
        for hh in range(2):
            update(hh, s_sc[slot, hh], mx_sc[slot, hh], vt_fn(hh))

    def k_at(j):
        o = pl.multiple_of(j * sub, sub)
        return lambda hh: k_ref[hh, pl.ds(o, sub), :]

    def vt_at(j):
        o = pl.multiple_of(j * sub, sub)
        return lambda hh: vt_ref[hh * VT_ROWS:(hh + 1) * VT_ROWS, pl.ds(o, sub)]

    scores(0, k_at(0))
    for hh in range(2):
        s = lax.dot_general(kc_ref[hh], q_ref[hh], (((1,), (1,)), ((), ())), preferred_element_type=F32)
        update(hh, s, jnp.max(s, axis=0, keepdims=True), vtc_ref[hh * VT_ROWS:(hh + 1) * VT_ROWS, :])

    def body(i, carry):
        j = 2 * i
        scores(1, k_at(j + 1))
        softmax_pv(0, vt_at(j))
        scores(0, k_at(jnp.minimum(j + 2, n_sub - 1)))
        softmax_pv(1, vt_at(j + 1))
        return carry

    lax.fori_loop(0, n_sub // 2, body, 0, unroll=math.gcd(ATTN_UNROLL, n_sub // 2))

    halves = []
    for hh in range(2):
        acc = acc_sc[hh]
        halves.append(acc[:V_DIM] * (1.0 / acc[V_DIM:V_DIM + 1]))
    o_ref[...] = jnp.concatenate(halves, axis=0).T.astype(BF16)


def _attention(q, k_ctx, vt_ctx, k_lat, vt_lat, tq):
    b_, nh, s_, hp = q.shape
    c_ = k_ctx.shape[2]
    assert s_ % (2 * ATTN_SUB) == 0
    return pl.pallas_call(
        _attn_kernel,
        grid=(b_, nh // 2, s_ // tq),
        in_specs=[
            pl.BlockSpec((None, 2, tq, hp), lambda b, h, i: (b, h, i, 0)),
            pl.BlockSpec((None, 2, c_, hp), lambda b, h, i: (b, h, 0, 0)),
            pl.BlockSpec((None, 2 * VT_ROWS, c_), lambda b, h, i: (b, h, 0)),
            pl.BlockSpec((None, 2, s_, hp), lambda b, h, i: (b, h, 0, 0)),
            pl.BlockSpec((None, 2 * VT_ROWS, s_), lambda b, h, i: (b, h, 0)),
        ],
        out_specs=pl.BlockSpec((None, tq, 2 * V_DIM), lambda b, h, i: (b, i, h)),
        out_shape=jax.ShapeDtypeStruct((b_, s_, nh * V_DIM), BF16),
        scratch_shapes=[pltpu.VMEM((2, 1, tq), F32), pltpu.VMEM((2, VT_ROWS, tq), F32),
                        pltpu.VMEM((2, 2, ATTN_SUB, tq), F32), pltpu.VMEM((2, 2, 1, tq), F32)],
        compiler_params=_cparams(("parallel", "parallel", "arbitrary")),
        name="mla_attention",
    )(q, k_ctx, vt_ctx, k_lat, vt_lat)


def _gate_blocks(gate_w_d):
    hd = LRU_HEAD_DIM
    per = LRU_GROUP // hd
    ng = LRU_HEADS // per
    w = gate_w_d.reshape(2, ng, per, hd, hd)
    eye = jnp.eye(per, dtype=gate_w_d.dtype)
    blk = jnp.einsum('gnpij,pq->gnpiqj', w, eye).reshape(2, ng, LRU_GROUP, LRU_GROUP)
    return (0.5 * jnp.concatenate([blk[0], blk[1]], axis=-1)).astype(BF16)


def _pad_heads(w, per_head, offset=0):
    k = w.shape[0]
    w3 = w.reshape(k, MLA_HEADS, per_head)
    out = jnp.zeros((k, MLA_HEADS, HEAD_PAD), w.dtype)
    out = out.at[:, :, offset:offset + per_head].set(w3)
    return out.reshape(k, MLA_HEADS * HEAD_PAD)


def _rope_swap_idx():
    return np.arange(QK_ROPE) ^ ROPE_PAIRS


def _prep_mla(w_in, w_uq, w_ukv):
    d = w_in.shape[0]
    qc = w_in[:, :Q_LORA]
    kvc = w_in[:, Q_LORA:Q_LORA + KV_LORA]
    kr = w_in[:, Q_LORA + KV_LORA:Q_LORA + KV_LORA + QK_ROPE]
    g = w_in[:, Q_LORA + KV_LORA + QK_ROPE:]
    swap = _rope_swap_idx()
    kr_blk = jnp.zeros((d, HEAD_PAD), w_in.dtype).at[:, QK_NOPE:QK_NOPE + QK_ROPE].set(kr)
    krs_blk = jnp.zeros((d, HEAD_PAD), w_in.dtype).at[:, QK_NOPE:QK_NOPE + QK_ROPE].set(kr[:, swap])
    w_in_p = jnp.concatenate([qc, kvc, kr_blk, krs_blk, g], axis=1).astype(BF16)

    wq3 = w_uq.reshape(Q_LORA, MLA_HEADS, QK_DIM)
    wq_p = _pad_heads(w_uq, QK_DIM).astype(BF16)
    rope_sw = wq3[:, :, QK_NOPE:][:, :, swap]
    wqs = jnp.zeros((Q_LORA, MLA_HEADS, HEAD_PAD), w_uq.dtype).at[:, :, QK_NOPE:QK_NOPE + QK_ROPE].set(rope_sw)
    wqs_p = wqs.reshape(Q_LORA, MLA_HEADS * HEAD_PAD).astype(BF16)

    wkv3 = w_ukv.reshape(KV_LORA, MLA_HEADS, QK_NOPE + V_DIM)
    wk_p = _pad_heads(wkv3[:, :, :QK_NOPE].reshape(KV_LORA, -1), QK_NOPE).astype(BF16)
    wv3 = jnp.transpose(wkv3[:, :, QK_NOPE:], (1, 2, 0))
    wvt_p = jnp.pad(wv3, ((0, 0), (0, VT_ROWS - V_DIM), (0, 0))).reshape(MLA_HEADS * VT_ROWS, KV_LORA)
    return w_in_p, wq_p, wqs_p, wk_p, wvt_p.astype(BF16)


def _rope_tables(s_):
    rows = s_ // GRID_W
    f32 = np.float32
    inv = f32(ROPE_THETA) ** (-np.arange(ROPE_PAIRS, dtype=f32) / f32(ROPE_PAIRS))
    a_r = np.arange(rows, dtype=f32)[:, None] * inv
    a_c = np.arange(GRID_W, dtype=f32)[:, None] * inv
    shape = (rows, GRID_W, ROPE_PAIRS)
    by_row = lambda t: np.broadcast_to(t[:, None, :], shape)
    by_col = lambda t: np.broadcast_to(t[None, :, :], shape)
    cr, sr, cc, sc = by_row(np.cos(a_r)), by_row(np.sin(a_r)), by_col(np.cos(a_c)), by_col(np.sin(a_c))
    pad = HEAD_PAD - QK_NOPE - QK_ROPE
    cos_t = np.concatenate([np.ones(shape[:2] + (QK_NOPE,), f32), cr, cr, cc, cc,
                            np.ones(shape[:2] + (pad,), f32)], axis=-1)
    sin_t = np.concatenate([np.zeros(shape[:2] + (QK_NOPE,), f32), -sr, sr, -sc, sc,
                            np.zeros(shape[:2] + (pad,), f32)], axis=-1)
    return jnp.asarray(cos_t.reshape(s_, HEAD_PAD), F32), jnp.asarray(sin_t.reshape(s_, HEAD_PAD), F32)


def kernel(x, c, ctx, c_ctx, ada_w, ada_b, ln_g, ln_b, w_in_rf, conv_w, conv_b, lru_gate_w, lru_gate_b,
           lru_lambda, fnet_w, fnet_b, w_out_rf, w_in_mla, q_norm_g, kv_norm_g, w_uq, w_ukv, w_out_mla):
    b_, s_, d = x.shape
    c_len = ctx.shape[1]
    tm = 1024 if s_ % 1024 == 0 else s_
    t_scan = 512 if s_ % 512 == 0 else s_

    cond = jnp.zeros((SUBLANES, d), F32).at[:b_].set(c).at[b_].set(c_ctx)
    mod = _adaln(cond, ada_w, ada_b).reshape(DEPTH, SUBLANES, 3, d)
    mod_lat = [mod[l, :b_] for l in range(DEPTH)]
    mod_ctx = [jnp.broadcast_to(mod[l, b_][None], (b_, 3, d)) for l in range(DEPTH)]

    w_in0 = w_in_rf[0].astype(BF16)
    w_out0 = w_out_rf[0].astype(BF16)
    fw_bf = fnet_w[0].astype(BF16)
    wg = [_gate_blocks(lru_gate_w[0, dd]) for dd in range(2)]
    cb = conv_b[0].reshape(1, -1)

    zl_c, zf_c, sg_c = _in0(ctx, mod_ctx[0], w_in0, c_len)
    zl_l, zf_l, sg_l = _in0(x, mod_lat[0], w_in0, tm)

    h_c, h_l = [], []
    for dd in range(2):
        rev = dd == 1
        lam = lru_lambda[0, dd].reshape(1, -1)
        zero = jnp.zeros((b_, 1, d), F32)
        hc, hc_fin = _lru_scan(zl_c, conv_w[0], cb, wg[dd], lru_gate_b[0, dd], lam, zero,
                               reverse=rev, t_rows=c_len)
        hl, _ = _lru_scan(zl_l, conv_w[0], cb, wg[dd], lru_gate_b[0, dd], lam, hc_fin,
                          reverse=rev, t_rows=t_scan)
        h_c.append(hc)
        h_l.append(hl)

    f_c = _fnet_ctx(zf_c, fw_bf, fnet_b[0])
    f_l = _fnet_lat(zf_l, fw_bf, fnet_b[0])

    lg0, lb0 = ln_g[0].reshape(1, d), ln_b[0].reshape(1, d)
    h1_c = _out0(h_c[0], h_c[1], f_c, sg_c, ctx, mod_ctx[0], w_out0, lg0, lb0, c_len)
    h1_l = _out0(h_l[0], h_l[1], f_l, sg_l, x, mod_lat[0], w_out0, lg0, lb0, tm)

    w_in_p, wq_p, wqs_p, wk_p, wvt_p = _prep_mla(w_in_mla[0], w_uq[0], w_ukv[0])
    qg = q_norm_g[0].reshape(1, -1)
    kg = kv_norm_g[0].reshape(1, -1)
    cos_t, sin_t = _rope_tables(s_)

    k_c, v_c = _in1(h1_c, mod_ctx[1], w_in_p, qg, kg, wq_p, wqs_p, wk_p, wvt_p, cos_t, sin_t, c_len,
                    want_q=False)
    q_l, k_l, v_l, sg1 = _in1(h1_l, mod_lat[1], w_in_p, qg, kg, wq_p, wqs_p, wk_p, wvt_p, cos_t, sin_t, tm,
                              want_q=True)

    tq = 512 if s_ % 1024 == 0 else s_
    o = _attention(q_l, k_c, v_c, k_l, v_l, tq)

    lg1, lb1 = ln_g[1].reshape(1, d), ln_b[1].reshape(1, d)
    return _out1(o, sg1, h1_l, mod_lat[1], w_out_mla[0].astype(BF16), lg1, lb1, tm)
```

```python
import functools
import math

import numpy as np
import jax
import jax.numpy as jnp
from jax import lax
from jax.experimental import pallas as pl
from jax.experimental.pallas import tpu as pltpu

F32 = jnp.float32
BF16 = jnp.bfloat16

DEPTH = 2
GRID_W = 64
DEEPNORM_ALPHA = (2 * DEPTH) ** 0.25
LN_EPS = 1e-6
RMS_EPS = 1e-6
LRU_HEADS = 16
LRU_HEAD_DIM = 64
LRU_GROUP = 256
CONV_W = 4
LRU_C = 8.0
FNET_GROUPS = 4
FNET_GROUP_DIM = 128
MLA_HEADS = 16
Q_LORA = 256
KV_LORA = 128
QK_NOPE = 64
QK_ROPE = 32
V_DIM = 64
QK_DIM = QK_NOPE + QK_ROPE
ROPE_PAIRS = QK_ROPE // 4
ROPE_THETA = 10000.0
ATTN_SCALE = QK_DIM ** -0.5
LOG2E = 1.4426950408889634

LANES = 128
SUBLANES = 8
BF16_ROWS = 16
VMEM_LIMIT = 56 * 1024 * 1024

HEAD_PAD = LANES
VT_ROWS = V_DIM + BF16_ROWS
ATTN_SUB = 256
ATTN_UNROLL = 16
ATTN_MAX_JUMP = 60.0
FFT_N2 = 128
FFT_PITCH = FFT_N2 + SUBLANES
FFT_BATCH = 16
NEG_BIG = -1e30


def _sigmoid(x):
    return 0.5 * (jnp.tanh(0.5 * x) + 1.0)


def _silu(x):
    return x * _sigmoid(x)


def _cparams(sem):
    return pltpu.CompilerParams(dimension_semantics=sem, vmem_limit_bytes=VMEM_LIMIT)


def _adaln_kernel(cond_ref, w_ref, b_ref, o_ref):
    c = cond_ref[...]
    o_ref[...] = jnp.dot(_silu(c), w_ref[...], preferred_element_type=F32,
                         precision=lax.Precision.HIGHEST) + b_ref[...]


def _adaln(cond, ada_w, ada_b):
    depth, d, n = ada_w.shape
    rows = cond.shape[0]
    tn = 768
    return pl.pallas_call(
        _adaln_kernel,
        grid=(depth, n // tn),
        in_specs=[
            pl.BlockSpec((rows, d), lambda l, j: (0, 0)),
            pl.BlockSpec((None, d, tn), lambda l, j: (l, 0, j)),
            pl.BlockSpec((None, 1, tn), lambda l, j: (l, 0, j)),
        ],
        out_specs=pl.BlockSpec((None, rows, tn), lambda l, j: (l, 0, j)),
        out_shape=jax.ShapeDtypeStruct((depth, rows, n), F32),
        compiler_params=_cparams(("parallel", "parallel")),
        name="adaln",
    )(cond, ada_w, ada_b.reshape(depth, 1, n))


def _in0_kernel(x_ref, mod_ref, w_ref, zl_ref, zf_ref, sg_ref, *, lru_w, fn_w):
    shift = mod_ref[0:1, :]
    scale = mod_ref[1:2, :]
    u = (x_ref[...] * (1.0 + scale) + shift).astype(BF16)
    zl_ref[...] = jnp.dot(u, w_ref[:, :lru_w], preferred_element_type=F32).astype(BF16)
    zf_ref[...] = jnp.dot(u, w_ref[:, lru_w:lru_w + fn_w], preferred_element_type=F32).astype(BF16)
    g0 = lru_w + fn_w
    for c0 in range(0, lru_w + fn_w, 512):
        g = jnp.dot(u, w_ref[:, g0 + c0:g0 + c0 + 512], preferred_element_type=F32)
        sg_ref[:, c0:c0 + 512] = _silu(g).astype(BF16)


def _in0(x, mod, w_bf, tm):
    b_, l_, d = x.shape
    lru_w, fn_w = d, d // 2
    mix = lru_w + fn_w
    return pl.pallas_call(
        functools.partial(_in0_kernel, lru_w=lru_w, fn_w=fn_w),
        grid=(b_, l_ // tm),
        in_specs=[
            pl.BlockSpec((None, tm, d), lambda b, i: (b, i, 0)),
            pl.BlockSpec((None, 3, d), lambda b, i: (b, 0, 0)),
            pl.BlockSpec((d, 2 * mix), lambda b, i: (0, 0)),
        ],
        out_specs=[
            pl.BlockSpec((None, tm, lru_w), lambda b, i: (b, i, 0)),
            pl.BlockSpec((None, tm, fn_w), lambda b, i: (b, i, 0)),
            pl.BlockSpec((None, tm, mix), lambda b, i: (b, i, 0)),
        ],
        out_shape=[
            jax.ShapeDtypeStruct((b_, l_, lru_w), BF16),
            jax.ShapeDtypeStruct((b_, l_, fn_w), BF16),
            jax.ShapeDtypeStruct((b_, l_, mix), BF16),
        ],
        compiler_params=_cparams(("parallel", "parallel")),
        name="in_proj0",
    )(x, mod, w_bf)


def _scan_kernel(z_ref, zp_ref, zn_ref, cw_ref, cb_ref, wg_ref, gb_ref, lam_ref, h0_ref,
                 h_ref, hfin_ref, zz_sc, a_sc, b_sc, hs_sc, carry_sc, *, reverse, t_rows):
    i = pl.program_id(0)
    nt = pl.num_programs(0)
    tt = (nt - 1 - i) if reverse else i
    halo = BF16_ROWS
    nb = z_ref.shape[0]
    width = z_ref.shape[-1]

    @pl.when(i == 0)
    def _():
        carry_sc[...] = jnp.broadcast_to(h0_ref[...], carry_sc.shape)

    nl = -lam_ref[...]
    half_c = (0.5 * LRU_C) * (jnp.maximum(nl, 0.0) + jnp.log(1.0 + jnp.exp(-jnp.abs(nl))))
    half_gb = 0.5 * gb_ref[...]
    n_buf = t_rows + 2 * halo
    for bi in range(nb):
        zz_sc[0:halo, :] = jnp.where(tt > 0, zp_ref[bi].astype(F32), 0.0)
        zz_sc[halo:halo + t_rows, :] = z_ref[bi].astype(F32)
        zz_sc[halo + t_rows:halo + t_rows + halo, :] = jnp.where(tt < nt - 1, zn_ref[bi].astype(F32), 0.0)
        for j in range(width // LRU_GROUP):
            cs = slice(j * LRU_GROUP, (j + 1) * LRU_GROUP)
            zf = zz_sc[:, cs]
            xc = cb_ref[:, cs] + cw_ref[1:2, cs] * zf[halo:halo + t_rows]
            for k, shift in ((0, 1), (2, n_buf - 1), (3, n_buf - 2)):
                xc = xc + cw_ref[k:k + 1, cs] * pltpu.roll(zf, shift, axis=0)[halo:halo + t_rows]
            g = jnp.dot(xc.astype(BF16), wg_ref[j], preferred_element_type=F32)
            t_r = jnp.tanh(g[:, :LRU_GROUP] + half_gb[0:1, cs])
            t_i = jnp.tanh(g[:, LRU_GROUP:] + half_gb[1:2, cs])
            hc = half_c[:, cs]
            nla = hc * t_r + hc
            a = jnp.exp2(nla * (-LOG2E))
            y = jnp.tanh(nla) * (1.0 + a * a)
            sq = jnp.where(y > 0.0, y * lax.rsqrt(y), 0.0)
            hx = 0.5 * xc
            a_sc[bi, :, cs] = a
            b_sc[bi, :, cs] = sq * (hx * t_i + hx)

    def group(gi, hs):
        base = (t_rows - SUBLANES - gi * SUBLANES) if reverse else gi * SUBLANES
        base = pl.multiple_of(base, SUBLANES)
        hs = list(hs)
        for t in range(SUBLANES):
            row = base + ((SUBLANES - 1 - t) if reverse else t)
            for bi in range(nb):
                hs[bi] = a_sc[bi, pl.ds(row, 1), :] * hs[bi] + b_sc[bi, pl.ds(row, 1), :]
                hs_sc[bi, pl.ds(row, 1), :] = hs[bi]
        return tuple(hs)

    h_last = lax.fori_loop(0, t_rows // SUBLANES, group, tuple(carry_sc[bi, 0:1, :] for bi in range(nb)))
    for bi in range(nb):
        carry_sc[bi, 0:1, :] = h_last[bi]
    h_ref[...] = hs_sc[...].astype(BF16)

    @pl.when(i == nt - 1)
    def _():
        for bi in range(nb):
            hfin_ref[bi] = h_last[bi]


def _lru_scan(zl, conv_w, conv_b, wg_d, gb_d, lam_d, h0_d, *, reverse, t_rows):
    b_, l_, w = zl.shape
    nt = l_ // t_rows
    hb = t_rows // BF16_ROWS
    n_hb = l_ // BF16_ROWS

    def tile(i):
        return (nt - 1 - i) if reverse else i

    return pl.pallas_call(
        functools.partial(_scan_kernel, reverse=reverse, t_rows=t_rows),
        grid=(nt,),
        in_specs=[
            pl.BlockSpec((b_, t_rows, w), lambda i: (0, tile(i), 0)),
            pl.BlockSpec((b_, BF16_ROWS, w), lambda i: (0, jnp.maximum(tile(i) * hb - 1, 0), 0)),
            pl.BlockSpec((b_, BF16_ROWS, w), lambda i: (0, jnp.minimum((tile(i) + 1) * hb, n_hb - 1), 0)),
            pl.BlockSpec((CONV_W, w), lambda i: (0, 0)),
            pl.BlockSpec((1, w), lambda i: (0, 0)),
            pl.BlockSpec((w // LRU_GROUP, LRU_GROUP, 2 * LRU_GROUP), lambda i: (0, 0, 0)),
            pl.BlockSpec((2, w), lambda i: (0, 0)),
            pl.BlockSpec((1, w), lambda i: (0, 0)),
            pl.BlockSpec((b_, 1, w), lambda i: (0, 0, 0)),
        ],
        out_specs=[
            pl.BlockSpec((b_, t_rows, w), lambda i: (0, tile(i), 0)),
            pl.BlockSpec((b_, 1, w), lambda i: (0, 0, 0)),
        ],
        out_shape=[
            jax.ShapeDtypeStruct((b_, l_, w), BF16),
            jax.ShapeDtypeStruct((b_, 1, w), F32),
        ],
        scratch_shapes=[
            pltpu.VMEM((t_rows + 2 * BF16_ROWS, w), F32),
            pltpu.VMEM((b_, t_rows, w), F32),
            pltpu.VMEM((b_, t_rows, w), F32),
            pltpu.VMEM((b_, t_rows, w), F32),
            pltpu.VMEM((b_, SUBLANES, w), F32),
        ],
        compiler_params=_cparams(("arbitrary",)),
        name="lru_scan_bwd" if reverse else "lru_scan_fwd",
    )(zl, zl, zl, conv_w, conv_b, wg_d, gb_d, lam_d, h0_d)


def _dft_cos_sin(n):
    k = np.arange(n, dtype=np.int64)
    ang = 2.0 * np.pi * ((k[:, None] * k[None, :]) % n).astype(np.float64) / n
    return np.cos(ang), np.sin(ang)


def _mxu_const(a):
    return jnp.asarray(a, F32).astype(BF16)


def _fnet_lat_kernel(x_ref, fc_ref, m1_ref, c2s2_ref, twa_ref, twb_ref, fw_ref, fb_ref, o_ref, s_sc,
                     *, n1, norm):
    n2 = FFT_N2
    pitch = FFT_PITCH
    gd = x_ref.shape[-1]

    nb = FFT_BATCH

    def stage0(i, carry):
        r0 = pl.multiple_of(i * (nb * n2), nb * n2)
        u = jnp.dot(x_ref[pl.ds(r0, nb * n2), :], fc_ref[...], preferred_element_type=F32)
        for j in range(nb):
            d0 = pl.multiple_of((i * nb + j) * pitch, SUBLANES)
            s_sc[0, pl.ds(d0, n2), :] = u[j * n2:(j + 1) * n2, :gd]
            s_sc[1, pl.ds(d0, n2), :] = u[j * n2:(j + 1) * n2, gd:]
        return carry

    lax.fori_loop(0, n1 // nb, stage0, 0)

    def stage1(a, carry):
        ta_r = twa_ref[0, a]
        ta_i = twa_ref[1, a]
        gs = []
        for b in range(SUBLANES):
            col = a * SUBLANES + b
            g_r = s_sc[0, pl.ds(col, n1, stride=pitch), :]
            g_i = s_sc[1, pl.ds(col, n1, stride=pitch), :]
            gs.append(jnp.concatenate([g_r, g_i], axis=0).astype(BF16))
        y_all = jnp.dot(m1_ref[...], jnp.concatenate(gs, axis=1), preferred_element_type=F32)
        for b in range(SUBLANES):
            col = a * SUBLANES + b
            y_r = y_all[:n1, b * gd:(b + 1) * gd]
            y_i = y_all[n1:, b * gd:(b + 1) * gd]
            tb_r = twb_ref[0, b]
            tb_i = twb_ref[1, b]
            c = ta_r * tb_r - ta_i * tb_i
            s = -(ta_r * tb_i + ta_i * tb_r)
            s_sc[0, pl.ds(col, n1, stride=pitch), :] = y_r * c + y_i * s
            s_sc[1, pl.ds(col, n1, stride=pitch), :] = y_i * c - y_r * s
        return carry

    lax.fori_loop(0, n2 // SUBLANES, stage1, 0)

    def stage2(i, carry):
        zs = []
        for j in range(nb):
            d0 = pl.multiple_of((i * nb + j) * pitch, SUBLANES)
            zs.append(jnp.concatenate([s_sc[0, pl.ds(d0, n2), :], s_sc[1, pl.ds(d0, n2), :]],
                                      axis=0).astype(BF16))
        z = jnp.concatenate(zs, axis=1)
        xr = jnp.dot(c2s2_ref[...], z, preferred_element_type=F32) * norm
        xs = jnp.concatenate([xr[:, j * gd:(j + 1) * gd] for j in range(nb)], axis=0).astype(BF16)
        out = jnp.dot(xs, fw_ref[...], preferred_element_type=F32) + fb_ref[...]
        for j in range(nb):
            o_ref[pl.ds(i * nb + j, n2, stride=n1), :] = out[j * n2:(j + 1) * n2, :]
        return carry

    lax.fori_loop(0, n1 // nb, stage2, 0)


def _fnet_lat(zf, fnet_w_bf, fnet_b):
    b_, l_, fw = zf.shape
    gd = FNET_GROUP_DIM
    groups = fw // gd
    n2 = FFT_N2
    n1 = l_ // n2
    assert n1 * n2 == l_ and n1 % SUBLANES == 0
    cc, sc = _dft_cos_sin(gd)
    fc = _mxu_const(np.concatenate([cc, -sc], axis=1))
    c1, s1 = _dft_cos_sin(n1)
    m1 = _mxu_const(np.block([[c1, s1], [-s1, c1]]))
    c2, s2 = _dft_cos_sin(n2)
    c2s2 = _mxu_const(np.concatenate([c2, s2], axis=1))
    k1 = np.arange(n1, dtype=np.float64)[None, :, None]
    ang_a = 2.0 * np.pi * (np.arange(n2 // SUBLANES, dtype=np.float64) * SUBLANES)[:, None, None] * k1 / l_
    ang_b = 2.0 * np.pi * np.arange(SUBLANES, dtype=np.float64)[:, None, None] * k1 / l_
    ones = np.ones((1, 1, gd))
    twa = jnp.asarray(np.stack([np.cos(ang_a) * ones, -np.sin(ang_a) * ones]), F32)
    twb = jnp.asarray(np.stack([np.cos(ang_b) * ones, -np.sin(ang_b) * ones]), F32)
    norm = 1.0 / math.sqrt(l_ * gd)
    const2 = lambda b, g: (0, 0)
    const4 = lambda b, g: (0, 0, 0, 0)
    return pl.pallas_call(
        functools.partial(_fnet_lat_kernel, n1=n1, norm=norm),
        grid=(b_, groups),
        in_specs=[
            pl.BlockSpec((None, l_, gd), lambda b, g: (b, 0, g)),
            pl.BlockSpec(fc.shape, const2),
            pl.BlockSpec(m1.shape, const2),
            pl.BlockSpec(c2s2.shape, const2),
            pl.BlockSpec(twa.shape, const4),
            pl.BlockSpec(twb.shape, const4),
            pl.BlockSpec((None, gd, gd), lambda b, g: (g, 0, 0)),
            pl.BlockSpec((None, 1, gd), lambda b, g: (g, 0, 0)),
        ],
        out_specs=pl.BlockSpec((None, l_, gd), lambda b, g: (b, 0, g)),
        out_shape=jax.ShapeDtypeStruct((b_, l_, fw), F32),
        scratch_shapes=[pltpu.VMEM((2, n1 * FFT_PITCH, gd), F32)],
        compiler_params=_cparams(("parallel", "parallel")),
        name="fnet_lat",
    )(zf, fc, m1, c2s2, twa, twb, fnet_w_bf, fnet_b.reshape(groups, 1, gd))


def _fnet_ctx_kernel(x_ref, fc_ref, cs_ref, fw_ref, fb_ref, o_ref, *, norm):
    gd = x_ref.shape[-1]
    u = jnp.dot(x_ref[...], fc_ref[...], preferred_element_type=F32)
    z = jnp.concatenate([u[:, :gd], u[:, gd:]], axis=0).astype(BF16)
    xr = jnp.dot(cs_ref[...], z, preferred_element_type=F32) * norm
    o_ref[...] = jnp.dot(xr.astype(BF16), fw_ref[...], preferred_element_type=F32) + fb_ref[...]


def _fnet_ctx(zf, fnet_w_bf, fnet_b):
    b_, l_, fw = zf.shape
    gd = FNET_GROUP_DIM
    groups = fw // gd
    cc, sc = _dft_cos_sin(gd)
    fc = _mxu_const(np.concatenate([cc, -sc], axis=1))
    cl, sl = _dft_cos_sin(l_)
    cs = _mxu_const(np.concatenate([cl, sl], axis=1))
    norm = 1.0 / math.sqrt(l_ * gd)
    const2 = lambda b, g: (0, 0)
    return pl.pallas_call(
        functools.partial(_fnet_ctx_kernel, norm=norm),
        grid=(b_, groups),
        in_specs=[
            pl.BlockSpec((None, l_, gd), lambda b, g: (b, 0, g)),
            pl.BlockSpec(fc.shape, const2),
            pl.BlockSpec(cs.shape, const2),
            pl.BlockSpec((None, gd, gd), lambda b, g: (g, 0, 0)),
            pl.BlockSpec((None, 1, gd), lambda b, g: (g, 0, 0)),
        ],
        out_specs=pl.BlockSpec((None, l_, gd), lambda b, g: (b, 0, g)),
        out_shape=jax.ShapeDtypeStruct((b_, l_, fw), F32),
        compiler_params=_cparams(("parallel", "parallel")),
        name="fnet_ctx",
    )(zf, fc, cs, fnet_w_bf, fnet_b.reshape(groups, 1, gd))


def _deepnorm(resid, gate, y, g, b):
    v = DEEPNORM_ALPHA * resid + gate * y
    mu = jnp.mean(v, axis=-1, keepdims=True)
    vc = v - mu
    var = jnp.mean(vc * vc, axis=-1, keepdims=True)
    return vc * lax.rsqrt(var + LN_EPS) * g + b


def _out0_kernel(hf_ref, hb_ref, f_ref, sg_ref, x_ref, mod_ref, w_ref, lg_ref, lb_ref, o_ref, *, lru_w):
    r = hf_ref[...].astype(F32) + hb_ref[...].astype(F32)
    m_l = (r * sg_ref[:, :lru_w].astype(F32)).astype(BF16)
    m_f = (f_ref[...] * sg_ref[:, lru_w:].astype(F32)).astype(BF16)
    y = jnp.dot(m_l, w_ref[:lru_w, :], preferred_element_type=F32)
    y = y + jnp.dot(m_f, w_ref[lru_w:, :], preferred_element_type=F32)
    o_ref[...] = _deepnorm(x_ref[...], mod_ref[2:3, :], y, lg_ref[...], lb_ref[...])


def _out0(hf, hb, f, sg, x, mod, w_bf, ln_g, ln_b, tm):
    b_, l_, d = x.shape
    lru_w = hf.shape[-1]
    fn_w = f.shape[-1]
    mix = lru_w + fn_w
    row = lambda b, i: (b, i, 0)
    const2 = lambda b, i: (0, 0)
    return pl.pallas_call(
        functools.partial(_out0_kernel, lru_w=lru_w),
        grid=(b_, l_ // tm),
        in_specs=[
            pl.BlockSpec((None, tm, lru_w), row),
            pl.BlockSpec((None, tm, lru_w), row),
            pl.BlockSpec((None, tm, fn_w), row),
            pl.BlockSpec((None, tm, mix), row),
            pl.BlockSpec((None, tm, d), row),
            pl.BlockSpec((None, 3, d), lambda b, i: (b, 0, 0)),
            pl.BlockSpec((mix, d), const2),
            pl.BlockSpec((1, d), const2),
            pl.BlockSpec((1, d), const2),
        ],
        out_specs=pl.BlockSpec((None, tm, d), row),
        out_shape=jax.ShapeDtypeStruct((b_, l_, d), F32),
        compiler_params=_cparams(("parallel", "parallel")),
        name="out_proj0",
    )(hf, hb, f, sg, x, mod, w_bf, ln_g, ln_b)


def _out1_kernel(o_ref_in, sg_ref, x_ref, mod_ref, w_ref, lg_ref, lb_ref, o_ref):
    m = (o_ref_in[...].astype(F32) * sg_ref[...].astype(F32)).astype(BF16)
    y = jnp.dot(m, w_ref[...], preferred_element_type=F32)
    o_ref[...] = _deepnorm(x_ref[...], mod_ref[2:3, :], y, lg_ref[...], lb_ref[...])


def _out1(o, sg, x, mod, w_bf, ln_g, ln_b, tm):
    b_, l_, d = x.shape
    wdt = o.shape[-1]
    row = lambda b, i: (b, i, 0)
    const2 = lambda b, i: (0, 0)
    return pl.pallas_call(
        _out1_kernel,
        grid=(b_, l_ // tm),
        in_specs=[
            pl.BlockSpec((None, tm, wdt), row),
            pl.BlockSpec((None, tm, wdt), row),
            pl.BlockSpec((None, tm, d), row),
            pl.BlockSpec((None, 3, d), lambda b, i: (b, 0, 0)),
            pl.BlockSpec((wdt, d), const2),
            pl.BlockSpec((1, d), const2),
            pl.BlockSpec((1, d), const2),
        ],
        out_specs=pl.BlockSpec((None, tm, d), row),
        out_shape=jax.ShapeDtypeStruct((b_, l_, d), F32),
        compiler_params=_cparams(("parallel", "parallel")),
        name="out_proj1",
    )(o, sg, x, mod, w_bf, ln_g, ln_b)


def _rms(x, g):
    return x * lax.rsqrt(jnp.mean(x * x, axis=-1, keepdims=True) + RMS_EPS) * g


def _in1_kernel(*refs, rope, want_q):
    if want_q:
        (x_ref, mod_ref, w_ref, qg_ref, kg_ref, wq_ref, wqs_ref, wk_ref, wvt_ref, cos_ref, sin_ref,
         q_ref, k_ref, vt_ref, sg_ref) = refs
    else:
        (x_ref, mod_ref, w_ref, kg_ref, wk_ref, wvt_ref, k_ref, vt_ref) = refs
    shift = mod_ref[0:1, :]
    scale = mod_ref[1:2, :]
    u = (x_ref[...] * (1.0 + scale) + shift).astype(BF16)
    hp = HEAD_PAD
    o_kv = Q_LORA
    o_kr = Q_LORA + KV_LORA
    o_g = o_kr + 2 * hp

    kvn = _rms(jnp.dot(u, w_ref[:, o_kv:o_kv + KV_LORA], preferred_element_type=F32), kg_ref[...]).astype(BF16)
    kr = jnp.dot(u, w_ref[:, o_kr:o_kr + 2 * hp], preferred_element_type=F32)
    if rope:
        cos = cos_ref[...]
        sin = sin_ref[...]
        krope = kr[:, :hp] * cos + kr[:, hp:] * sin
    else:
        krope = kr[:, :hp]
    for h2 in range(MLA_HEADS // 2):
        cs = slice(2 * h2 * hp, (2 * h2 + 2) * hp)
        kn = jnp.dot(kvn, wk_ref[:, cs], preferred_element_type=F32)
        k_ref[2 * h2] = (kn[:, :hp] + krope).astype(BF16)
        k_ref[2 * h2 + 1] = (kn[:, hp:] + krope).astype(BF16)
    vt = lax.dot_general(wvt_ref[...], kvn, (((1,), (1,)), ((), ())), preferred_element_type=F32)
    row = lax.broadcasted_iota(jnp.int32, (vt.shape[0], 1), 0)
    vt_ref[...] = (vt + (row % VT_ROWS == V_DIM).astype(F32)).astype(BF16)

    if want_q:
        qn = _rms(jnp.dot(u, w_ref[:, :Q_LORA], preferred_element_type=F32), qg_ref[...]).astype(BF16)
        qscale = ATTN_SCALE * LOG2E
        cq = cos * qscale
        sq = sin * qscale
        for h2 in range(MLA_HEADS // 2):
            cs = slice(2 * h2 * hp, (2 * h2 + 2) * hp)
            z1 = jnp.dot(qn, wq_ref[:, cs], preferred_element_type=F32)
            z2 = jnp.dot(qn, wqs_ref[:, cs], preferred_element_type=F32)
            q_ref[2 * h2] = (z1[:, :hp] * cq + z2[:, :hp] * sq).astype(BF16)
            q_ref[2 * h2 + 1] = (z1[:, hp:] * cq + z2[:, hp:] * sq).astype(BF16)
        gw = sg_ref.shape[-1]
        for c0 in range(0, gw, 512):
            g = jnp.dot(u, w_ref[:, o_g + c0:o_g + c0 + 512], preferred_element_type=F32)
            sg_ref[:, c0:c0 + 512] = _silu(g).astype(BF16)


def _in1(x, mod, w_in_p, qg, kg, wq_p, wqs_p, wk_p, wvt_p, cos_t, sin_t, tm, *, want_q):
    b_, l_, d = x.shape
    hp = HEAD_PAD
    nh = MLA_HEADS
    gw = nh * V_DIM
    row = lambda b, i: (b, i, 0)
    const2 = lambda b, i: (0, 0)
    head = lambda b, i: (b, 0, i, 0)
    kv_specs = [pl.BlockSpec((None, nh, tm, hp), head),
                pl.BlockSpec((None, nh * VT_ROWS, tm), lambda b, i: (b, 0, i))]
    kv_shapes = [jax.ShapeDtypeStruct((b_, nh, l_, hp), BF16),
                 jax.ShapeDtypeStruct((b_, nh * VT_ROWS, l_), BF16)]
    if want_q:
        ins = [x, mod, w_in_p, qg, kg, wq_p, wqs_p, wk_p, wvt_p, cos_t, sin_t]
        in_specs = [
            pl.BlockSpec((None, tm, d), row),
            pl.BlockSpec((None, 3, d), lambda b, i: (b, 0, 0)),
            pl.BlockSpec(w_in_p.shape, const2),
            pl.BlockSpec(qg.shape, const2),
            pl.BlockSpec(kg.shape, const2),
            pl.BlockSpec(wq_p.shape, const2),
            pl.BlockSpec(wqs_p.shape, const2),
            pl.BlockSpec(wk_p.shape, const2),
            pl.BlockSpec(wvt_p.shape, const2),
            pl.BlockSpec((tm, hp), lambda b, i: (i, 0)),
            pl.BlockSpec((tm, hp), lambda b, i: (i, 0)),
        ]
        out_specs = [pl.BlockSpec((None, nh, tm, hp), head)] + kv_specs + [pl.BlockSpec((None, tm, gw), row)]
        out_shape = [jax.ShapeDtypeStruct((b_, nh, l_, hp), BF16)] + kv_shapes + [
            jax.ShapeDtypeStruct((b_, l_, gw), BF16)]
    else:
        ins = [x, mod, w_in_p, kg, wk_p, wvt_p]
        in_specs = [
            pl.BlockSpec((None, tm, d), row),
            pl.BlockSpec((None, 3, d), lambda b, i: (b, 0, 0)),
            pl.BlockSpec(w_in_p.shape, const2),
            pl.BlockSpec(kg.shape, const2),
            pl.BlockSpec(wk_p.shape, const2),
            pl.BlockSpec(wvt_p.shape, const2),
        ]
        out_specs = kv_specs
        out_shape = kv_shapes
    return pl.pallas_call(
        functools.partial(_in1_kernel, rope=want_q, want_q=want_q),
        grid=(b_, l_ // tm),
        in_specs=in_specs,
        out_specs=out_specs,
        out_shape=out_shape,
        compiler_params=_cparams(("parallel", "parallel")),
        name="in_proj1" if want_q else "in_proj1_ctx",
    )(*ins)


def _attn_kernel(q_ref, kc_ref, vtc_ref, k_ref, vt_ref, o_ref, m_sc, acc_sc, p_sc, ref_sc, racc_sc, jump_sc):
    sub = ATTN_SUB
    n_sub = k_ref.shape[1] // sub

    def qk(k, hh):
        return lax.dot_general(k, q_ref[hh], (((1,), (1,)), ((), ())), preferred_element_type=F32)

    def update(hh, s, vt):
        m_prev = m_sc[hh]
        m_new = jnp.maximum(m_prev, jnp.max(s, axis=0, keepdims=True))
        alpha = jnp.exp2(m_prev - m_new)
        p = jnp.exp2(s - m_new).astype(BF16)
        acc_sc[hh] = alpha * acc_sc[hh] + jnp.dot(vt, p, preferred_element_type=F32)
        m_sc[hh] = m_new
        racc_sc[hh] = m_new

    def start():
        m_sc[...] = jnp.full(m_sc.shape, NEG_BIG, F32)
        acc_sc[...] = jnp.zeros(acc_sc.shape, F32)
        jump_sc[...] = jnp.zeros(jump_sc.shape, F32)
        for hh in range(2):
            update(hh, qk(kc_ref[hh], hh), vtc_ref[hh * VT_ROWS:(hh + 1) * VT_ROWS, :])

    def k_at(j, hh):
        return k_ref[hh, pl.ds(pl.multiple_of(j * sub, sub), sub), :]

    def vt_at(j, hh):
        return vt_ref[hh * VT_ROWS:(hh + 1) * VT_ROWS, pl.ds(pl.multiple_of(j * sub, sub), sub)]

    def scores(slot, j):
        for hh in range(2):
            s = qk(k_at(j, hh), hh)
            r = m_sc[hh]
            p_sc[slot, hh] = jnp.exp2(s - r).astype(BF16)
            ref_sc[slot, hh] = r
            m_new = jnp.maximum(r, jnp.max(s, axis=0, keepdims=True))
            jump_sc[hh] = jnp.maximum(jump_sc[hh], m_new - r)
            m_sc[hh] = m_new

    def values(slot, j):
        for hh in range(2):
            r_p = ref_sc[slot, hh]
            alpha = jnp.exp2(racc_sc[hh] - r_p)
            acc_sc[hh] = alpha * acc_sc[hh] + jnp.dot(vt_at(j, hh), p_sc[slot, hh], preferred_element_type=F32)
            racc_sc[hh] = r_p

    start()
    scores(0, 0)

    def body(i, carry):
        j = 2 * i
        scores(1, j + 1)
        values(0, j)
        scores(0, jnp.minimum(j + 2, n_sub - 1))
        values(1, j + 1)
        return carry

    lax.fori_loop(0, n_sub // 2, body, 0, unroll=math.gcd(ATTN_UNROLL, n_sub // 2))

    @pl.when(jnp.max(jump_sc[...]) > ATTN_MAX_JUMP)
    def _():
        start()

        def safe_body(j, carry):
            for hh in range(2):
                update(hh, qk(k_at(j, hh), hh), vt_at(j, hh))
            return carry

        lax.fori_loop(0, n_sub, safe_body, 0)

    halves = []
    for hh in range(2):
        acc = acc_sc[hh]
        halves.append(acc[:V_DIM] * (1.0 / acc[V_DIM:V_DIM + 1]))
    o_ref[...] = jnp.concatenate(halves, axis=0).T.astype(BF16)


def _attention(q, k_ctx, vt_ctx, k_lat, vt_lat, tq):
    b_, nh, s_, hp = q.shape
    c_ = k_ctx.shape[2]
    assert s_ % (2 * ATTN_SUB) == 0
    return pl.pallas_call(
        _attn_kernel,
        grid=(b_, nh // 2, s_ // tq),
        in_specs=[
            pl.BlockSpec((None, 2, tq, hp), lambda b, h, i: (b, h, i, 0)),
            pl.BlockSpec((None, 2, c_, hp), lambda b, h, i: (b, h, 0, 0)),
            pl.BlockSpec((None, 2 * VT_ROWS, c_), lambda b, h, i: (b, h, 0)),
            pl.BlockSpec((None, 2, s_, hp), lambda b, h, i: (b, h, 0, 0)),
            pl.BlockSpec((None, 2 * VT_ROWS, s_), lambda b, h, i: (b, h, 0)),
        ],
        out_specs=pl.BlockSpec((None, tq, 2 * V_DIM), lambda b, h, i: (b, i, h)),
        out_shape=jax.ShapeDtypeStruct((b_, s_, nh * V_DIM), BF16),
        scratch_shapes=[pltpu.VMEM((2, 1, tq), F32), pltpu.VMEM((2, VT_ROWS, tq), F32),
                        pltpu.VMEM((2, 2, ATTN_SUB, tq), BF16), pltpu.VMEM((2, 2, 1, tq), F32),
                        pltpu.VMEM((2, 1, tq), F32), pltpu.VMEM((2, 1, tq), F32)],
        compiler_params=_cparams(("parallel", "parallel", "arbitrary")),
        name="mla_attention",
    )(q, k_ctx, vt_ctx, k_lat, vt_lat)


def _gate_blocks(gate_w_d):
    hd = LRU_HEAD_DIM
    per = LRU_GROUP // hd
    ng = LRU_HEADS // per
    w = gate_w_d.reshape(2, ng, per, hd, hd)
    eye = jnp.eye(per, dtype=gate_w_d.dtype)
    blk = jnp.einsum('gnpij,pq->gnpiqj', w, eye).reshape(2, ng, LRU_GROUP, LRU_GROUP)
    return (0.5 * jnp.concatenate([blk[0], blk[1]], axis=-1)).astype(BF16)


def _pad_heads(w, per_head, offset=0):
    k = w.shape[0]
    w3 = w.reshape(k, MLA_HEADS, per_head)
    out = jnp.zeros((k, MLA_HEADS, HEAD_PAD), w.dtype)
    out = out.at[:, :, offset:offset + per_head].set(w3)
    return out.reshape(k, MLA_HEADS * HEAD_PAD)


def _rope_swap_idx():
    return np.arange(QK_ROPE) ^ ROPE_PAIRS


def _prep_mla(w_in, w_uq, w_ukv):
    d = w_in.shape[0]
    qc = w_in[:, :Q_LORA]
    kvc = w_in[:, Q_LORA:Q_LORA + KV_LORA]
    kr = w_in[:, Q_LORA + KV_LORA:Q_LORA + KV_LORA + QK_ROPE]
    g = w_in[:, Q_LORA + KV_LORA + QK_ROPE:]
    swap = _rope_swap_idx()
    kr_blk = jnp.zeros((d, HEAD_PAD), w_in.dtype).at[:, QK_NOPE:QK_NOPE + QK_ROPE].set(kr)
    krs_blk = jnp.zeros((d, HEAD_PAD), w_in.dtype).at[:, QK_NOPE:QK_NOPE + QK_ROPE].set(kr[:, swap])
    w_in_p = jnp.concatenate([qc, kvc, kr_blk, krs_blk, g], axis=1).astype(BF16)

    wq3 = w_uq.reshape(Q_LORA, MLA_HEADS, QK_DIM)
    wq_p = _pad_heads(w_uq, QK_DIM).astype(BF16)
    rope_sw = wq3[:, :, QK_NOPE:][:, :, swap]
    wqs = jnp.zeros((Q_LORA, MLA_HEADS, HEAD_PAD), w_uq.dtype).at[:, :, QK_NOPE:QK_NOPE + QK_ROPE].set(rope_sw)
    wqs_p = wqs.reshape(Q_LORA, MLA_HEADS * HEAD_PAD).astype(BF16)

    wkv3 = w_ukv.reshape(KV_LORA, MLA_HEADS, QK_NOPE + V_DIM)
    wk_p = _pad_heads(wkv3[:, :, :QK_NOPE].reshape(KV_LORA, -1), QK_NOPE).astype(BF16)
    wv3 = jnp.transpose(wkv3[:, :, QK_NOPE:], (1, 2, 0))
    wvt_p = jnp.pad(wv3, ((0, 0), (0, VT_ROWS - V_DIM), (0, 0))).reshape(MLA_HEADS * VT_ROWS, KV_LORA)
    return w_in_p, wq_p, wqs_p, wk_p, wvt_p.astype(BF16)


def _rope_tables(s_):
    rows = s_ // GRID_W
    f32 = np.float32
    inv = f32(ROPE_THETA) ** (-np.arange(ROPE_PAIRS, dtype=f32) / f32(ROPE_PAIRS))
    a_r = np.arange(rows, dtype=f32)[:, None] * inv
    a_c = np.arange(GRID_W, dtype=f32)[:, None] * inv
    shape = (rows, GRID_W, ROPE_PAIRS)
    by_row = lambda t: np.broadcast_to(t[:, None, :], shape)
    by_col = lambda t: np.broadcast_to(t[None, :, :], shape)
    cr, sr, cc, sc = by_row(np.cos(a_r)), by_row(np.sin(a_r)), by_col(np.cos(a_c)), by_col(np.sin(a_c))
    pad = HEAD_PAD - QK_NOPE - QK_ROPE
    cos_t = np.concatenate([np.ones(shape[:2] + (QK_NOPE,), f32), cr, cr, cc, cc,
                            np.ones(shape[:2] + (pad,), f32)], axis=-1)
    sin_t = np.concatenate([np.zeros(shape[:2] + (QK_NOPE,), f32), -sr, sr, -sc, sc,
                            np.zeros(shape[:2] + (pad,), f32)], axis=-1)
    return jnp.asarray(cos_t.reshape(s_, HEAD_PAD), F32), jnp.asarray(sin_t.reshape(s_, HEAD_PAD), F32)


def kernel(x, c, ctx, c_ctx, ada_w, ada_b, ln_g, ln_b, w_in_rf, conv_w, conv_b, lru_gate_w, lru_gate_b,
           lru_lambda, fnet_w, fnet_b, w_out_rf, w_in_mla, q_norm_g, kv_norm_g, w_uq, w_ukv, w_out_mla):
    b_, s_, d = x.shape
    c_len = ctx.shape[1]
    tm = 1024 if s_ % 1024 == 0 else s_
    t_scan = 512 if s_ % 512 == 0 else s_

    cond = jnp.zeros((SUBLANES, d), F32).at[:b_].set(c).at[b_].set(c_ctx)
    mod = _adaln(cond, ada_w, ada_b).reshape(DEPTH, SUBLANES, 3, d)
    mod_lat = [mod[l, :b_] for l in range(DEPTH)]
    mod_ctx = [jnp.broadcast_to(mod[l, b_][None], (b_, 3, d)) for l in range(DEPTH)]

    w_in0 = w_in_rf[0].astype(BF16)
    w_out0 = w_out_rf[0].astype(BF16)
    fw_bf = fnet_w[0].astype(BF16)
    wg = [_gate_blocks(lru_gate_w[0, dd]) for dd in range(2)]
    cb = conv_b[0].reshape(1, -1)

    zl_c, zf_c, sg_c = _in0(ctx, mod_ctx[0], w_in0, c_len)
    zl_l, zf_l, sg_l = _in0(x, mod_lat[0], w_in0, tm)

    h_c, h_l = [], []
    for dd in range(2):
        rev = dd == 1
        lam = lru_lambda[0, dd].reshape(1, -1)
        zero = jnp.zeros((b_, 1, d), F32)
        hc, hc_fin = _lru_scan(zl_c, conv_w[0], cb, wg[dd], lru_gate_b[0, dd], lam, zero,
                               reverse=rev, t_rows=c_len)
        hl, _ = _lru_scan(zl_l, conv_w[0], cb, wg[dd], lru_gate_b[0, dd], lam, hc_fin,
                          reverse=rev, t_rows=t_scan)
        h_c.append(hc)
        h_l.append(hl)

    f_c = _fnet_ctx(zf_c, fw_bf, fnet_b[0])
    f_l = _fnet_lat(zf_l, fw_bf, fnet_b[0])

    lg0, lb0 = ln_g[0].reshape(1, d), ln_b[0].reshape(1, d)
    h1_c = _out0(h_c[0], h_c[1], f_c, sg_c, ctx, mod_ctx[0], w_out0, lg0, lb0, c_len)
    h1_l = _out0(h_l[0], h_l[1], f_l, sg_l, x, mod_lat[0], w_out0, lg0, lb0, tm)

    w_in_p, wq_p, wqs_p, wk_p, wvt_p = _prep_mla(w_in_mla[0], w_uq[0], w_ukv[0])
    qg = q_norm_g[0].reshape(1, -1)
    kg = kv_norm_g[0].reshape(1, -1)
    cos_t, sin_t = _rope_tables(s_)

    k_c, v_c = _in1(h1_c, mod_ctx[1], w_in_p, qg, kg, wq_p, wqs_p, wk_p, wvt_p, cos_t, sin_t, c_len,
                    want_q=False)
    q_l, k_l, v_l, sg1 = _in1(h1_l, mod_lat[1], w_in_p, qg, kg, wq_p, wqs_p, wk_p, wvt_p, cos_t, sin_t, tm,
                              want_q=True)

    tq = 512 if s_ % 1024 == 0 else s_
    o = _attention(q_l, k_c, v_c, k_l, v_l, tq)

    lg1, lb1 = ln_g[1].reshape(1, d), ln_b[1].reshape(1, d)
    return _out1(o, sg1, h1_l, mod_lat[1], w_out_mla[0].astype(BF16), lg1, lb1, tm)
```

```python
import functools
import math

import numpy as np
import jax
import jax.numpy as jnp
from jax import lax
from jax.experimental import pallas as pl
from jax.experimental.pallas import tpu as pltpu

F32 = jnp.float32
BF16 = jnp.bfloat16

DEPTH = 2
GRID_W = 64
DEEPNORM_ALPHA = (2 * DEPTH) ** 0.25
LN_EPS = 1e-6
RMS_EPS = 1e-6
LRU_HEADS = 16
LRU_HEAD_DIM = 64
LRU_GROUP = 256
CONV_W = 4
LRU_C = 8.0
FNET_GROUPS = 4
FNET_GROUP_DIM = 128
MLA_HEADS = 16
Q_LORA = 256
KV_LORA = 128
QK_NOPE = 64
QK_ROPE = 32
V_DIM = 64
QK_DIM = QK_NOPE + QK_ROPE
ROPE_PAIRS = QK_ROPE // 4
ROPE_THETA = 10000.0
ATTN_SCALE = QK_DIM ** -0.5
LOG2E = 1.4426950408889634

LANES = 128
SUBLANES = 8
BF16_ROWS = 16
VMEM_LIMIT = 56 * 1024 * 1024

HEAD_PAD = LANES
VT_ROWS = V_DIM + BF16_ROWS
ATTN_SUB = 256
ATTN_UNROLL = 4
ATTN_MAX_JUMP = 60.0
FFT_N2 = 128
FFT_PITCH = FFT_N2 + SUBLANES
FFT_BATCH = 16
NEG_BIG = -1e30


def _sigmoid(x):
    return 0.5 * (jnp.tanh(0.5 * x) + 1.0)


def _silu(x):
    return x * _sigmoid(x)


def _cparams(sem):
    return pltpu.CompilerParams(dimension_semantics=sem, vmem_limit_bytes=VMEM_LIMIT)


def _adaln_kernel(cond_ref, w_ref, b_ref, o_ref):
    c = cond_ref[...]
    o_ref[...] = jnp.dot(_silu(c), w_ref[...], preferred_element_type=F32,
                         precision=lax.Precision.HIGHEST) + b_ref[...]


def _adaln(cond, ada_w, ada_b):
    depth, d, n = ada_w.shape
    rows = cond.shape[0]
    tn = 768
    return pl.pallas_call(
        _adaln_kernel,
        grid=(depth, n // tn),
        in_specs=[
            pl.BlockSpec((rows, d), lambda l, j: (0, 0)),
            pl.BlockSpec((None, d, tn), lambda l, j: (l, 0, j)),
            pl.BlockSpec((None, 1, tn), lambda l, j: (l, 0, j)),
        ],
        out_specs=pl.BlockSpec((None, rows, tn), lambda l, j: (l, 0, j)),
        out_shape=jax.ShapeDtypeStruct((depth, rows, n), F32),
        compiler_params=_cparams(("parallel", "parallel")),
        name="adaln",
    )(cond, ada_w, ada_b.reshape(depth, 1, n))


def _in0_kernel(x_ref, mod_ref, w_ref, zl_ref, zf_ref, sg_ref, *, lru_w, fn_w):
    shift = mod_ref[0:1, :]
    scale = mod_ref[1:2, :]
    u = (x_ref[...] * (1.0 + scale) + shift).astype(BF16)
    zl_ref[...] = jnp.dot(u, w_ref[:, :lru_w], preferred_element_type=F32).astype(BF16)
    zf_ref[...] = jnp.dot(u, w_ref[:, lru_w:lru_w + fn_w], preferred_element_type=F32).astype(BF16)
    g0 = lru_w + fn_w
    for c0 in range(0, lru_w + fn_w, 512):
        g = jnp.dot(u, w_ref[:, g0 + c0:g0 + c0 + 512], preferred_element_type=F32)
        sg_ref[:, c0:c0 + 512] = _silu(g).astype(BF16)


def _in0(x, mod, w_bf, tm):
    b_, l_, d = x.shape
    lru_w, fn_w = d, d // 2
    mix = lru_w + fn_w
    return pl.pallas_call(
        functools.partial(_in0_kernel, lru_w=lru_w, fn_w=fn_w),
        grid=(b_, l_ // tm),
        in_specs=[
            pl.BlockSpec((None, tm, d), lambda b, i: (b, i, 0)),
            pl.BlockSpec((None, 3, d), lambda b, i: (b, 0, 0)),
            pl.BlockSpec((d, 2 * mix), lambda b, i: (0, 0)),
        ],
        out_specs=[
            pl.BlockSpec((None, tm, lru_w), lambda b, i: (b, i, 0)),
            pl.BlockSpec((None, tm, fn_w), lambda b, i: (b, i, 0)),
            pl.BlockSpec((None, tm, mix), lambda b, i: (b, i, 0)),
        ],
        out_shape=[
            jax.ShapeDtypeStruct((b_, l_, lru_w), BF16),
            jax.ShapeDtypeStruct((b_, l_, fn_w), BF16),
            jax.ShapeDtypeStruct((b_, l_, mix), BF16),
        ],
        compiler_params=_cparams(("parallel", "parallel")),
        name="in_proj0",
    )(x, mod, w_bf)


def _scan_kernel(z_ref, zp_ref, zn_ref, cw_ref, cb_ref, wg_ref, gb_ref, lam_ref, h0_ref,
                 h_ref, hfin_ref, zz_sc, a_sc, b_sc, hs_sc, carry_sc, *, reverse, t_rows):
    i = pl.program_id(0)
    nt = pl.num_programs(0)
    tt = (nt - 1 - i) if reverse else i
    halo = BF16_ROWS
    nb = z_ref.shape[0]
    width = z_ref.shape[-1]

    @pl.when(i == 0)
    def _():
        carry_sc[...] = jnp.broadcast_to(h0_ref[...], carry_sc.shape)

    nl = -lam_ref[...]
    half_c = (0.5 * LRU_C) * (jnp.maximum(nl, 0.0) + jnp.log(1.0 + jnp.exp(-jnp.abs(nl))))
    half_gb = 0.5 * gb_ref[...]
    n_buf = t_rows + 2 * halo
    for bi in range(nb):
        zz_sc[0:halo, :] = jnp.where(tt > 0, zp_ref[bi].astype(F32), 0.0)
        zz_sc[halo:halo + t_rows, :] = z_ref[bi].astype(F32)
        zz_sc[halo + t_rows:halo + t_rows + halo, :] = jnp.where(tt < nt - 1, zn_ref[bi].astype(F32), 0.0)
        for j in range(width // LRU_GROUP):
            cs = slice(j * LRU_GROUP, (j + 1) * LRU_GROUP)
            zf = zz_sc[:, cs]
            xc = cb_ref[:, cs] + cw_ref[1:2, cs] * zf[halo:halo + t_rows]
            for k, shift in ((0, 1), (2, n_buf - 1), (3, n_buf - 2)):
                xc = xc + cw_ref[k:k + 1, cs] * pltpu.roll(zf, shift, axis=0)[halo:halo + t_rows]
            g = jnp.dot(xc.astype(BF16), wg_ref[j], preferred_element_type=F32)
            t_r = jnp.tanh(g[:, :LRU_GROUP] + half_gb[0:1, cs])
            t_i = jnp.tanh(g[:, LRU_GROUP:] + half_gb[1:2, cs])
            hc = half_c[:, cs]
            nla = hc * t_r + hc
            a = jnp.exp2(nla * (-LOG2E))
            y = jnp.tanh(nla) * (1.0 + a * a)
            sq = jnp.where(y > 0.0, y * lax.rsqrt(y), 0.0)
            hx = 0.5 * xc
            a_sc[bi, :, cs] = a
            b_sc[bi, :, cs] = sq * (hx * t_i + hx)

    def group(gi, hs):
        base = (t_rows - SUBLANES - gi * SUBLANES) if reverse else gi * SUBLANES
        base = pl.multiple_of(base, SUBLANES)
        hs = list(hs)
        for t in range(SUBLANES):
            row = base + ((SUBLANES - 1 - t) if reverse else t)
            for bi in range(nb):
                hs[bi] = a_sc[bi, pl.ds(row, 1), :] * hs[bi] + b_sc[bi, pl.ds(row, 1), :]
                hs_sc[bi, pl.ds(row, 1), :] = hs[bi]
        return tuple(hs)

    h_last = lax.fori_loop(0, t_rows // SUBLANES, group, tuple(carry_sc[bi, 0:1, :] for bi in range(nb)))
    for bi in range(nb):
        carry_sc[bi, 0:1, :] = h_last[bi]
    h_ref[...] = hs_sc[...].astype(BF16)

    @pl.when(i == nt - 1)
    def _():
        for bi in range(nb):
            hfin_ref[bi] = h_last[bi]


def _lru_scan(zl, conv_w, conv_b, wg_d, gb_d, lam_d, h0_d, *, reverse, t_rows):
    b_, l_, w = zl.shape
    nt = l_ // t_rows
    hb = t_rows // BF16_ROWS
    n_hb = l_ // BF16_ROWS

    def tile(i):
        return (nt - 1 - i) if reverse else i

    return pl.pallas_call(
        functools.partial(_scan_kernel, reverse=reverse, t_rows=t_rows),
        grid=(nt,),
        in_specs=[
            pl.BlockSpec((b_, t_rows, w), lambda i: (0, tile(i), 0)),
            pl.BlockSpec((b_, BF16_ROWS, w), lambda i: (0, jnp.maximum(tile(i) * hb - 1, 0), 0)),
            pl.BlockSpec((b_, BF16_ROWS, w), lambda i: (0, jnp.minimum((tile(i) + 1) * hb, n_hb - 1), 0)),
            pl.BlockSpec((CONV_W, w), lambda i: (0, 0)),
            pl.BlockSpec((1, w), lambda i: (0, 0)),
            pl.BlockSpec((w // LRU_GROUP, LRU_GROUP, 2 * LRU_GROUP), lambda i: (0, 0, 0)),
            pl.BlockSpec((2, w), lambda i: (0, 0)),
            pl.BlockSpec((1, w), lambda i: (0, 0)),
            pl.BlockSpec((b_, 1, w), lambda i: (0, 0, 0)),
        ],
        out_specs=[
            pl.BlockSpec((b_, t_rows, w), lambda i: (0, tile(i), 0)),
            pl.BlockSpec((b_, 1, w), lambda i: (0, 0, 0)),
        ],
        out_shape=[
            jax.ShapeDtypeStruct((b_, l_, w), BF16),
            jax.ShapeDtypeStruct((b_, 1, w), F32),
        ],
        scratch_shapes=[
            pltpu.VMEM((t_rows + 2 * BF16_ROWS, w), F32),
            pltpu.VMEM((b_, t_rows, w), F32),
            pltpu.VMEM((b_, t_rows, w), F32),
            pltpu.VMEM((b_, t_rows, w), F32),
            pltpu.VMEM((b_, SUBLANES, w), F32),
        ],
        compiler_params=_cparams(("arbitrary",)),
        name="lru_scan_bwd" if reverse else "lru_scan_fwd",
    )(zl, zl, zl, conv_w, conv_b, wg_d, gb_d, lam_d, h0_d)


def _dft_cos_sin(n):
    k = np.arange(n, dtype=np.int64)
    ang = 2.0 * np.pi * ((k[:, None] * k[None, :]) % n).astype(np.float64) / n
    return np.cos(ang), np.sin(ang)


def _mxu_const(a):
    return jnp.asarray(a, F32).astype(BF16)


def _fnet_lat_kernel(x_ref, fc_ref, m1_ref, c2s2_ref, twa_ref, twb_ref, fw_ref, fb_ref, o_ref, s_sc,
                     *, n1, norm):
    n2 = FFT_N2
    pitch = FFT_PITCH
    gd = x_ref.shape[-1]

    nb = FFT_BATCH

    def stage0(i, carry):
        r0 = pl.multiple_of(i * (nb * n2), nb * n2)
        u = jnp.dot(x_ref[pl.ds(r0, nb * n2), :], fc_ref[...], preferred_element_type=F32)
        for j in range(nb):
            d0 = pl.multiple_of((i * nb + j) * pitch, SUBLANES)
            s_sc[0, pl.ds(d0, n2), :] = u[j * n2:(j + 1) * n2, :gd]
            s_sc[1, pl.ds(d0, n2), :] = u[j * n2:(j + 1) * n2, gd:]
        return carry

    lax.fori_loop(0, n1 // nb, stage0, 0)

    def stage1(a, carry):
        ta_r = twa_ref[0, a]
        ta_i = twa_ref[1, a]
        gs = []
        for b in range(SUBLANES):
            col = a * SUBLANES + b
            g_r = s_sc[0, pl.ds(col, n1, stride=pitch), :]
            g_i = s_sc[1, pl.ds(col, n1, stride=pitch), :]
            gs.append(jnp.concatenate([g_r, g_i], axis=0).astype(BF16))
        y_all = jnp.dot(m1_ref[...], jnp.concatenate(gs, axis=1), preferred_element_type=F32)
        for b in range(SUBLANES):
            col = a * SUBLANES + b
            y_r = y_all[:n1, b * gd:(b + 1) * gd]
            y_i = y_all[n1:, b * gd:(b + 1) * gd]
            tb_r = twb_ref[0, b]
            tb_i = twb_ref[1, b]
            c = ta_r * tb_r - ta_i * tb_i
            s = -(ta_r * tb_i + ta_i * tb_r)
            s_sc[0, pl.ds(col, n1, stride=pitch), :] = y_r * c + y_i * s
            s_sc[1, pl.ds(col, n1, stride=pitch), :] = y_i * c - y_r * s
        return carry

    lax.fori_loop(0, n2 // SUBLANES, stage1, 0)

    def stage2(i, carry):
        zs = []
        for j in range(nb):
            d0 = pl.multiple_of((i * nb + j) * pitch, SUBLANES)
            zs.append(jnp.concatenate([s_sc[0, pl.ds(d0, n2), :], s_sc[1, pl.ds(d0, n2), :]],
                                      axis=0).astype(BF16))
        z = jnp.concatenate(zs, axis=1)
        xr = jnp.dot(c2s2_ref[...], z, preferred_element_type=F32) * norm
        xs = jnp.concatenate([xr[:, j * gd:(j + 1) * gd] for j in range(nb)], axis=0).astype(BF16)
        out = jnp.dot(xs, fw_ref[...], preferred_element_type=F32) + fb_ref[...]
        for j in range(nb):
            o_ref[pl.ds(i * nb + j, n2, stride=n1), :] = out[j * n2:(j + 1) * n2, :]
        return carry

    lax.fori_loop(0, n1 // nb, stage2, 0)


def _fnet_lat(zf, fnet_w_bf, fnet_b):
    b_, l_, fw = zf.shape
    gd = FNET_GROUP_DIM
    groups = fw // gd
    n2 = FFT_N2
    n1 = l_ // n2
    assert n1 * n2 == l_ and n1 % SUBLANES == 0
    cc, sc = _dft_cos_sin(gd)
    fc = _mxu_const(np.concatenate([cc, -sc], axis=1))
    c1, s1 = _dft_cos_sin(n1)
    m1 = _mxu_const(np.block([[c1, s1], [-s1, c1]]))
    c2, s2 = _dft_cos_sin(n2)
    c2s2 = _mxu_const(np.concatenate([c2, s2], axis=1))
    k1 = np.arange(n1, dtype=np.float64)[None, :, None]
    ang_a = 2.0 * np.pi * (np.arange(n2 // SUBLANES, dtype=np.float64) * SUBLANES)[:, None, None] * k1 / l_
    ang_b = 2.0 * np.pi * np.arange(SUBLANES, dtype=np.float64)[:, None, None] * k1 / l_
    ones = np.ones((1, 1, gd))
    twa = jnp.asarray(np.stack([np.cos(ang_a) * ones, -np.sin(ang_a) * ones]), F32)
    twb = jnp.asarray(np.stack([np.cos(ang_b) * ones, -np.sin(ang_b) * ones]), F32)
    norm = 1.0 / math.sqrt(l_ * gd)
    const2 = lambda b, g: (0, 0)
    const4 = lambda b, g: (0, 0, 0, 0)
    return pl.pallas_call(
        functools.partial(_fnet_lat_kernel, n1=n1, norm=norm),
        grid=(b_, groups),
        in_specs=[
            pl.BlockSpec((None, l_, gd), lambda b, g: (b, 0, g)),
            pl.BlockSpec(fc.shape, const2),
            pl.BlockSpec(m1.shape, const2),
            pl.BlockSpec(c2s2.shape, const2),
            pl.BlockSpec(twa.shape, const4),
            pl.BlockSpec(twb.shape, const4),
            pl.BlockSpec((None, gd, gd), lambda b, g: (g, 0, 0)),
            pl.BlockSpec((None, 1, gd), lambda b, g: (g, 0, 0)),
        ],
        out_specs=pl.BlockSpec((None, l_, gd), lambda b, g: (b, 0, g)),
        out_shape=jax.ShapeDtypeStruct((b_, l_, fw), F32),
        scratch_shapes=[pltpu.VMEM((2, n1 * FFT_PITCH, gd), F32)],
        compiler_params=_cparams(("parallel", "parallel")),
        name="fnet_lat",
    )(zf, fc, m1, c2s2, twa, twb, fnet_w_bf, fnet_b.reshape(groups, 1, gd))


def _fnet_ctx_kernel(x_ref, fc_ref, cs_ref, fw_ref, fb_ref, o_ref, *, norm):
    gd = x_ref.shape[-1]
    u = jnp.dot(x_ref[...], fc_ref[...], preferred_element_type=F32)
    z = jnp.concatenate([u[:, :gd], u[:, gd:]], axis=0).astype(BF16)
    xr = jnp.dot(cs_ref[...], z, preferred_element_type=F32) * norm
    o_ref[...] = jnp.dot(xr.astype(BF16), fw_ref[...], preferred_element_type=F32) + fb_ref[...]


def _fnet_ctx(zf, fnet_w_bf, fnet_b):
    b_, l_, fw = zf.shape
    gd = FNET_GROUP_DIM
    groups = fw // gd
    cc, sc = _dft_cos_sin(gd)
    fc = _mxu_const(np.concatenate([cc, -sc], axis=1))
    cl, sl = _dft_cos_sin(l_)
    cs = _mxu_const(np.concatenate([cl, sl], axis=1))
    norm = 1.0 / math.sqrt(l_ * gd)
    const2 = lambda b, g: (0, 0)
    return pl.pallas_call(
        functools.partial(_fnet_ctx_kernel, norm=norm),
        grid=(b_, groups),
        in_specs=[
            pl.BlockSpec((None, l_, gd), lambda b, g: (b, 0, g)),
            pl.BlockSpec(fc.shape, const2),
            pl.BlockSpec(cs.shape, const2),
            pl.BlockSpec((None, gd, gd), lambda b, g: (g, 0, 0)),
            pl.BlockSpec((None, 1, gd), lambda b, g: (g, 0, 0)),
        ],
        out_specs=pl.BlockSpec((None, l_, gd), lambda b, g: (b, 0, g)),
        out_shape=jax.ShapeDtypeStruct((b_, l_, fw), F32),
        compiler_params=_cparams(("parallel", "parallel")),
        name="fnet_ctx",
    )(zf, fc, cs, fnet_w_bf, fnet_b.reshape(groups, 1, gd))


def _deepnorm(resid, gate, y, g, b):
    v = DEEPNORM_ALPHA * resid + gate * y
    mu = jnp.mean(v, axis=-1, keepdims=True)
    vc = v - mu
    var = jnp.mean(vc * vc, axis=-1, keepdims=True)
    return vc * lax.rsqrt(var + LN_EPS) * g + b


def _out0_kernel(hf_ref, hb_ref, f_ref, sg_ref, x_ref, mod_ref, w_ref, lg_ref, lb_ref, o_ref, *, lru_w):
    r = hf_ref[...].astype(F32) + hb_ref[...].astype(F32)
    m_l = (r * sg_ref[:, :lru_w].astype(F32)).astype(BF16)
    m_f = (f_ref[...] * sg_ref[:, lru_w:].astype(F32)).astype(BF16)
    y = jnp.dot(m_l, w_ref[:lru_w, :], preferred_element_type=F32)
    y = y + jnp.dot(m_f, w_ref[lru_w:, :], preferred_element_type=F32)
    o_ref[...] = _deepnorm(x_ref[...], mod_ref[2:3, :], y, lg_ref[...], lb_ref[...])


def _out0(hf, hb, f, sg, x, mod, w_bf, ln_g, ln_b, tm):
    b_, l_, d = x.shape
    lru_w = hf.shape[-1]
    fn_w = f.shape[-1]
    mix = lru_w + fn_w
    row = lambda b, i: (b, i, 0)
    const2 = lambda b, i: (0, 0)
    return pl.pallas_call(
        functools.partial(_out0_kernel, lru_w=lru_w),
        grid=(b_, l_ // tm),
        in_specs=[
            pl.BlockSpec((None, tm, lru_w), row),
            pl.BlockSpec((None, tm, lru_w), row),
            pl.BlockSpec((None, tm, fn_w), row),
            pl.BlockSpec((None, tm, mix), row),
            pl.BlockSpec((None, tm, d), row),
            pl.BlockSpec((None, 3, d), lambda b, i: (b, 0, 0)),
            pl.BlockSpec((mix, d), const2),
            pl.BlockSpec((1, d), const2),
            pl.BlockSpec((1, d), const2),
        ],
        out_specs=pl.BlockSpec((None, tm, d), row),
        out_shape=jax.ShapeDtypeStruct((b_, l_, d), F32),
        compiler_params=_cparams(("parallel", "parallel")),
        name="out_proj0",
    )(hf, hb, f, sg, x, mod, w_bf, ln_g, ln_b)


def _out1_kernel(o_ref_in, sg_ref, x_ref, mod_ref, w_ref, lg_ref, lb_ref, o_ref):
    m = (o_ref_in[...].astype(F32) * sg_ref[...].astype(F32)).astype(BF16)
    y = jnp.dot(m, w_ref[...], preferred_element_type=F32)
    o_ref[...] = _deepnorm(x_ref[...], mod_ref[2:3, :], y, lg_ref[...], lb_ref[...])


def _out1(o, sg, x, mod, w_bf, ln_g, ln_b, tm):
    b_, l_, d = x.shape
    wdt = o.shape[-1]
    row = lambda b, i: (b, i, 0)
    const2 = lambda b, i: (0, 0)
    return pl.pallas_call(
        _out1_kernel,
        grid=(b_, l_ // tm),
        in_specs=[
            pl.BlockSpec((None, tm, wdt), row),
            pl.BlockSpec((None, tm, wdt), row),
            pl.BlockSpec((None, tm, d), row),
            pl.BlockSpec((None, 3, d), lambda b, i: (b, 0, 0)),
            pl.BlockSpec((wdt, d), const2),
            pl.BlockSpec((1, d), const2),
            pl.BlockSpec((1, d), const2),
        ],
        out_specs=pl.BlockSpec((None, tm, d), row),
        out_shape=jax.ShapeDtypeStruct((b_, l_, d), F32),
        compiler_params=_cparams(("parallel", "parallel")),
        name="out_proj1",
    )(o, sg, x, mod, w_bf, ln_g, ln_b)


def _rms(x, g):
    return x * lax.rsqrt(jnp.mean(x * x, axis=-1, keepdims=True) + RMS_EPS) * g


def _in1_kernel(*refs, rope, want_q):
    if want_q:
        (x_ref, mod_ref, w_ref, qg_ref, kg_ref, wq_ref, wqs_ref, wk_ref, wvt_ref, cos_ref, sin_ref,
         q_ref, k_ref, vt_ref, sg_ref) = refs
    else:
        (x_ref, mod_ref, w_ref, kg_ref, wk_ref, wvt_ref, k_ref, vt_ref) = refs
    shift = mod_ref[0:1, :]
    scale = mod_ref[1:2, :]
    u = (x_ref[...] * (1.0 + scale) + shift).astype(BF16)
    hp = HEAD_PAD
    o_kv = Q_LORA
    o_kr = Q_LORA + KV_LORA
    o_g = o_kr + 2 * hp

    kvn = _rms(jnp.dot(u, w_ref[:, o_kv:o_kv + KV_LORA], preferred_element_type=F32), kg_ref[...]).astype(BF16)
    kr = jnp.dot(u, w_ref[:, o_kr:o_kr + 2 * hp], preferred_element_type=F32)
    if rope:
        cos = cos_ref[...]
        sin = sin_ref[...]
        krope = kr[:, :hp] * cos + kr[:, hp:] * sin
    else:
        krope = kr[:, :hp]
    for h2 in range(MLA_HEADS // 2):
        cs = slice(2 * h2 * hp, (2 * h2 + 2) * hp)
        kn = jnp.dot(kvn, wk_ref[:, cs], preferred_element_type=F32)
        k_ref[2 * h2] = (kn[:, :hp] + krope).astype(BF16)
        k_ref[2 * h2 + 1] = (kn[:, hp:] + krope).astype(BF16)
    vt = lax.dot_general(wvt_ref[...], kvn, (((1,), (1,)), ((), ())), preferred_element_type=F32)
    row = lax.broadcasted_iota(jnp.int32, (vt.shape[0], 1), 0)
    vt_ref[...] = (vt + (row % VT_ROWS == V_DIM).astype(F32)).astype(BF16)

    if want_q:
        qn = _rms(jnp.dot(u, w_ref[:, :Q_LORA], preferred_element_type=F32), qg_ref[...]).astype(BF16)
        qscale = ATTN_SCALE * LOG2E
        cq = cos * qscale
        sq = sin * qscale
        for h2 in range(MLA_HEADS // 2):
            cs = slice(2 * h2 * hp, (2 * h2 + 2) * hp)
            z1 = jnp.dot(qn, wq_ref[:, cs], preferred_element_type=F32)
            z2 = jnp.dot(qn, wqs_ref[:, cs], preferred_element_type=F32)
            q_ref[2 * h2] = (z1[:, :hp] * cq + z2[:, :hp] * sq).astype(BF16)
            q_ref[2 * h2 + 1] = (z1[:, hp:] * cq + z2[:, hp:] * sq).astype(BF16)
        gw = sg_ref.shape[-1]
        for c0 in range(0, gw, 512):
            g = jnp.dot(u, w_ref[:, o_g + c0:o_g + c0 + 512], preferred_element_type=F32)
            sg_ref[:, c0:c0 + 512] = _silu(g).astype(BF16)


def _in1(x, mod, w_in_p, qg, kg, wq_p, wqs_p, wk_p, wvt_p, cos_t, sin_t, tm, *, want_q):
    b_, l_, d = x.shape
    hp = HEAD_PAD
    nh = MLA_HEADS
    gw = nh * V_DIM
    row = lambda b, i: (b, i, 0)
    const2 = lambda b, i: (0, 0)
    head = lambda b, i: (b, 0, i, 0)
    kv_specs = [pl.BlockSpec((None, nh, tm, hp), head),
                pl.BlockSpec((None, nh * VT_ROWS, tm), lambda b, i: (b, 0, i))]
    kv_shapes = [jax.ShapeDtypeStruct((b_, nh, l_, hp), BF16),
                 jax.ShapeDtypeStruct((b_, nh * VT_ROWS, l_), BF16)]
    if want_q:
        ins = [x, mod, w_in_p, qg, kg, wq_p, wqs_p, wk_p, wvt_p, cos_t, sin_t]
        in_specs = [
            pl.BlockSpec((None, tm, d), row),
            pl.BlockSpec((None, 3, d), lambda b, i: (b, 0, 0)),
            pl.BlockSpec(w_in_p.shape, const2),
            pl.BlockSpec(qg.shape, const2),
            pl.BlockSpec(kg.shape, const2),
            pl.BlockSpec(wq_p.shape, const2),
            pl.BlockSpec(wqs_p.shape, const2),
            pl.BlockSpec(wk_p.shape, const2),
            pl.BlockSpec(wvt_p.shape, const2),
            pl.BlockSpec((tm, hp), lambda b, i: (i, 0)),
            pl.BlockSpec((tm, hp), lambda b, i: (i, 0)),
        ]
        out_specs = [pl.BlockSpec((None, nh, tm, hp), head)] + kv_specs + [pl.BlockSpec((None, tm, gw), row)]
        out_shape = [jax.ShapeDtypeStruct((b_, nh, l_, hp), BF16)] + kv_shapes + [
            jax.ShapeDtypeStruct((b_, l_, gw), BF16)]
    else:
        ins = [x, mod, w_in_p, kg, wk_p, wvt_p]
        in_specs = [
            pl.BlockSpec((None, tm, d), row),
            pl.BlockSpec((None, 3, d), lambda b, i: (b, 0, 0)),
            pl.BlockSpec(w_in_p.shape, const2),
            pl.BlockSpec(kg.shape, const2),
            pl.BlockSpec(wk_p.shape, const2),
            pl.BlockSpec(wvt_p.shape, const2),
        ]
        out_specs = kv_specs
        out_shape = kv_shapes
    return pl.pallas_call(
        functools.partial(_in1_kernel, rope=want_q, want_q=want_q),
        grid=(b_, l_ // tm),
        in_specs=in_specs,
        out_specs=out_specs,
        out_shape=out_shape,
        compiler_params=_cparams(("parallel", "parallel")),
        name="in_proj1" if want_q else "in_proj1_ctx",
    )(*ins)


def _attn_kernel(q_ref, kc_ref, vtc_ref, k_ref, vt_ref, o_ref, m_sc, acc_sc, p_sc, ref_sc, racc_sc, jump_sc):
    sub = ATTN_SUB
    n_sub = k_ref.shape[1] // sub

    def qk(k, hh):
        return lax.dot_general(k, q_ref[hh], (((1,), (1,)), ((), ())), preferred_element_type=F32)

    def update(hh, s, vt):
        m_prev = m_sc[hh]
        m_new = jnp.maximum(m_prev, jnp.max(s, axis=0, keepdims=True))
        alpha = jnp.exp2(m_prev - m_new)
        p = jnp.exp2(s - m_new).astype(BF16)
        acc_sc[hh] = alpha * acc_sc[hh] + jnp.dot(vt, p, preferred_element_type=F32)
        m_sc[hh] = m_new
        racc_sc[hh] = m_new

    def start():
        m_sc[...] = jnp.full(m_sc.shape, NEG_BIG, F32)
        acc_sc[...] = jnp.zeros(acc_sc.shape, F32)
        jump_sc[...] = jnp.zeros(jump_sc.shape, F32)
        for hh in range(2):
            update(hh, qk(kc_ref[hh], hh), vtc_ref[hh * VT_ROWS:(hh + 1) * VT_ROWS, :])

    def k_at(j, hh):
        return k_ref[hh, pl.ds(pl.multiple_of(j * sub, sub), sub), :]

    def vt_at(j, hh):
        return vt_ref[hh * VT_ROWS:(hh + 1) * VT_ROWS, pl.ds(pl.multiple_of(j * sub, sub), sub)]

    def scores(slot, j):
        for hh in range(2):
            s = qk(k_at(j, hh), hh)
            r = m_sc[hh]
            p_sc[slot, hh] = jnp.exp2(s - r).astype(BF16)
            ref_sc[slot, hh] = r
            m_new = jnp.maximum(r, jnp.max(s, axis=0, keepdims=True))
            jump_sc[hh] = jnp.maximum(jump_sc[hh], m_new - r)
            m_sc[hh] = m_new

    def values(slot, j):
        for hh in range(2):
            r_p = ref_sc[slot, hh]
            alpha = jnp.exp2(racc_sc[hh] - r_p)
            acc_sc[hh] = alpha * acc_sc[hh] + jnp.dot(vt_at(j, hh), p_sc[slot, hh], preferred_element_type=F32)
            racc_sc[hh] = r_p

    start()
    scores(0, 0)

    def body(i, carry):
        j = 2 * i
        scores(1, j + 1)
        values(0, j)
        scores(0, jnp.minimum(j + 2, n_sub - 1))
        values(1, j + 1)
        return carry

    lax.fori_loop(0, n_sub // 2, body, 0, unroll=math.gcd(ATTN_UNROLL, n_sub // 2))

    @pl.when(jnp.max(jump_sc[...]) > ATTN_MAX_JUMP)
    def _():
        start()

        def safe_body(j, carry):
            for hh in range(2):
                update(hh, qk(k_at(j, hh), hh), vt_at(j, hh))
            return carry

        lax.fori_loop(0, n_sub, safe_body, 0)

    halves = []
    for hh in range(2):
        acc = acc_sc[hh]
        halves.append(acc[:V_DIM] * (1.0 / acc[V_DIM:V_DIM + 1]))
    o_ref[...] = jnp.concatenate(halves, axis=0).T.astype(BF16)


def _attention(q, k_ctx, vt_ctx, k_lat, vt_lat, tq):
    b_, nh, s_, hp = q.shape
    c_ = k_ctx.shape[2]
    assert s_ % (2 * ATTN_SUB) == 0
    return pl.pallas_call(
        _attn_kernel,
        grid=(b_, nh // 2, s_ // tq),
        in_specs=[
            pl.BlockSpec((None, 2, tq, hp), lambda b, h, i: (b, h, i, 0)),
            pl.BlockSpec((None, 2, c_, hp), lambda b, h, i: (b, h, 0, 0)),
            pl.BlockSpec((None, 2 * VT_ROWS, c_), lambda b, h, i: (b, h, 0)),
            pl.BlockSpec((None, 2, s_, hp), lambda b, h, i: (b, h, 0, 0)),
            pl.BlockSpec((None, 2 * VT_ROWS, s_), lambda b, h, i: (b, h, 0)),
        ],
        out_specs=pl.BlockSpec((None, tq, 2 * V_DIM), lambda b, h, i: (b, i, h)),
        out_shape=jax.ShapeDtypeStruct((b_, s_, nh * V_DIM), BF16),
        scratch_shapes=[pltpu.VMEM((2, 1, tq), F32), pltpu.VMEM((2, VT_ROWS, tq), F32),
                        pltpu.VMEM((2, 2, ATTN_SUB, tq), BF16), pltpu.VMEM((2, 2, 1, tq), F32),
                        pltpu.VMEM((2, 1, tq), F32), pltpu.VMEM((2, 1, tq), F32)],
        compiler_params=_cparams(("parallel", "parallel", "arbitrary")),
        name="mla_attention",
    )(q, k_ctx, vt_ctx, k_lat, vt_lat)


def _gate_blocks(gate_w_d):
    hd = LRU_HEAD_DIM
    per = LRU_GROUP // hd
    ng = LRU_HEADS // per
    w = gate_w_d.reshape(2, ng, per, hd, hd)
    eye = jnp.eye(per, dtype=gate_w_d.dtype)
    blk = jnp.einsum('gnpij,pq->gnpiqj', w, eye).reshape(2, ng, LRU_GROUP, LRU_GROUP)
    return (0.5 * jnp.concatenate([blk[0], blk[1]], axis=-1)).astype(BF16)


def _pad_heads(w, per_head, offset=0):
    k = w.shape[0]
    w3 = w.reshape(k, MLA_HEADS, per_head)
    out = jnp.zeros((k, MLA_HEADS, HEAD_PAD), w.dtype)
    out = out.at[:, :, offset:offset + per_head].set(w3)
    return out.reshape(k, MLA_HEADS * HEAD_PAD)


def _rope_swap_idx():
    return np.arange(QK_ROPE) ^ ROPE_PAIRS


def _prep_mla(w_in, w_uq, w_ukv):
    d = w_in.shape[0]
    qc = w_in[:, :Q_LORA]
    kvc = w_in[:, Q_LORA:Q_LORA + KV_LORA]
    kr = w_in[:, Q_LORA + KV_LORA:Q_LORA + KV_LORA + QK_ROPE]
    g = w_in[:, Q_LORA + KV_LORA + QK_ROPE:]
    swap = _rope_swap_idx()
    kr_blk = jnp.zeros((d, HEAD_PAD), w_in.dtype).at[:, QK_NOPE:QK_NOPE + QK_ROPE].set(kr)
    krs_blk = jnp.zeros((d, HEAD_PAD), w_in.dtype).at[:, QK_NOPE:QK_NOPE + QK_ROPE].set(kr[:, swap])
    w_in_p = jnp.concatenate([qc, kvc, kr_blk, krs_blk, g], axis=1).astype(BF16)

    wq3 = w_uq.reshape(Q_LORA, MLA_HEADS, QK_DIM)
    wq_p = _pad_heads(w_uq, QK_DIM).astype(BF16)
    rope_sw = wq3[:, :, QK_NOPE:][:, :, swap]
    wqs = jnp.zeros((Q_LORA, MLA_HEADS, HEAD_PAD), w_uq.dtype).at[:, :, QK_NOPE:QK_NOPE + QK_ROPE].set(rope_sw)
    wqs_p = wqs.reshape(Q_LORA, MLA_HEADS * HEAD_PAD).astype(BF16)

    wkv3 = w_ukv.reshape(KV_LORA, MLA_HEADS, QK_NOPE + V_DIM)
    wk_p = _pad_heads(wkv3[:, :, :QK_NOPE].reshape(KV_LORA, -1), QK_NOPE).astype(BF16)
    wv3 = jnp.transpose(wkv3[:, :, QK_NOPE:], (1, 2, 0))
    wvt_p = jnp.pad(wv3, ((0, 0), (0, VT_ROWS - V_DIM), (0, 0))).reshape(MLA_HEADS * VT_ROWS, KV_LORA)
    return w_in_p, wq_p, wqs_p, wk_p, wvt_p.astype(BF16)


def _rope_tables(s_):
    rows = s_ // GRID_W
    f32 = np.float32
    inv = f32(ROPE_THETA) ** (-np.arange(ROPE_PAIRS, dtype=f32) / f32(ROPE_PAIRS))
    a_r = np.arange(rows, dtype=f32)[:, None] * inv
    a_c = np.arange(GRID_W, dtype=f32)[:, None] * inv
    shape = (rows, GRID_W, ROPE_PAIRS)
    by_row = lambda t: np.broadcast_to(t[:, None, :], shape)
    by_col = lambda t: np.broadcast_to(t[None, :, :], shape)
    cr, sr, cc, sc = by_row(np.cos(a_r)), by_row(np.sin(a_r)), by_col(np.cos(a_c)), by_col(np.sin(a_c))
    pad = HEAD_PAD - QK_NOPE - QK_ROPE
    cos_t = np.concatenate([np.ones(shape[:2] + (QK_NOPE,), f32), cr, cr, cc, cc,
                            np.ones(shape[:2] + (pad,), f32)], axis=-1)
    sin_t = np.concatenate([np.zeros(shape[:2] + (QK_NOPE,), f32), -sr, sr, -sc, sc,
                            np.zeros(shape[:2] + (pad,), f32)], axis=-1)
    return jnp.asarray(cos_t.reshape(s_, HEAD_PAD), F32), jnp.asarray(sin_t.reshape(s_, HEAD_PAD), F32)


def kernel(x, c, ctx, c_ctx, ada_w, ada_b, ln_g, ln_b, w_in_rf, conv_w, conv_b, lru_gate_w, lru_gate_b,
           lru_lambda, fnet_w, fnet_b, w_out_rf, w_in_mla, q_norm_g, kv_norm_g, w_uq, w_ukv, w_out_mla):
    b_, s_, d = x.shape
    c_len = ctx.shape[1]
    tm = 1024 if s_ % 1024 == 0 else s_
    t_scan = 512 if s_ % 512 == 0 else s_

    cond = jnp.zeros((SUBLANES, d), F32).at[:b_].set(c).at[b_].set(c_ctx)
    mod = _adaln(cond, ada_w, ada_b).reshape(DEPTH, SUBLANES, 3, d)
    mod_lat = [mod[l, :b_] for l in range(DEPTH)]
    mod_ctx = [jnp.broadcast_to(mod[l, b_][None], (b_, 3, d)) for l in range(DEPTH)]

    w_in0 = w_in_rf[0].astype(BF16)
    w_out0 = w_out_rf[0].astype(BF16)
    fw_bf = fnet_w[0].astype(BF16)
    wg = [_gate_blocks(lru_gate_w[0, dd]) for dd in range(2)]
    cb = conv_b[0].reshape(1, -1)

    zl_c, zf_c, sg_c = _in0(ctx, mod_ctx[0], w_in0, c_len)
    zl_l, zf_l, sg_l = _in0(x, mod_lat[0], w_in0, tm)

    h_c, h_l = [], []
    for dd in range(2):
        rev = dd == 1
        lam = lru_lambda[0, dd].reshape(1, -1)
        zero = jnp.zeros((b_, 1, d), F32)
        hc, hc_fin = _lru_scan(zl_c, conv_w[0], cb, wg[dd], lru_gate_b[0, dd], lam, zero,
                               reverse=rev, t_rows=c_len)
        hl, _ = _lru_scan(zl_l, conv_w[0], cb, wg[dd], lru_gate_b[0, dd], lam, hc_fin,
                          reverse=rev, t_rows=t_scan)
        h_c.append(hc)
        h_l.append(hl)

    f_c = _fnet_ctx(zf_c, fw_bf, fnet_b[0])
    f_l = _fnet_lat(zf_l, fw_bf, fnet_b[0])

    lg0, lb0 = ln_g[0].reshape(1, d), ln_b[0].reshape(1, d)
    h1_c = _out0(h_c[0], h_c[1], f_c, sg_c, ctx, mod_ctx[0], w_out0, lg0, lb0, c_len)
    h1_l = _out0(h_l[0], h_l[1], f_l, sg_l, x, mod_lat[0], w_out0, lg0, lb0, tm)

    w_in_p, wq_p, wqs_p, wk_p, wvt_p = _prep_mla(w_in_mla[0], w_uq[0], w_ukv[0])
    qg = q_norm_g[0].reshape(1, -1)
    kg = kv_norm_g[0].reshape(1, -1)
    cos_t, sin_t = _rope_tables(s_)

    k_c, v_c = _in1(h1_c, mod_ctx[1], w_in_p, qg, kg, wq_p, wqs_p, wk_p, wvt_p, cos_t, sin_t, c_len,
                    want_q=False)
    q_l, k_l, v_l, sg1 = _in1(h1_l, mod_lat[1], w_in_p, qg, kg, wq_p, wqs_p, wk_p, wvt_p, cos_t, sin_t, tm,
                              want_q=True)

    tq = 4096 if s_ % 4096 == 0 else s_
    o = _attention(q_l, k_c, v_c, k_l, v_l, tq)

    lg1, lb1 = ln_g[1].reshape(1, d), ln_b[1].reshape(1, d)
    return _out1(o, sg1, h1_l, mod_lat[1], w_out_mla[0].astype(BF16), lg1, lb1, tm)
```

```python
import functools
import math

import numpy as np
import jax
import jax.numpy as jnp
from jax import lax
from jax.experimental import pallas as pl
from jax.experimental.pallas import tpu as pltpu

F32 = jnp.float32
BF16 = jnp.bfloat16

DEPTH = 2
GRID_W = 64
DEEPNORM_ALPHA = (2 * DEPTH) ** 0.25
LN_EPS = 1e-6
RMS_EPS = 1e-6
LRU_HEADS = 16
LRU_HEAD_DIM = 64
LRU_GROUP = 256
CONV_W = 4
LRU_C = 8.0
FNET_GROUPS = 4
FNET_GROUP_DIM = 128
MLA_HEADS = 16
Q_LORA = 256
KV_LORA = 128
QK_NOPE = 64
QK_ROPE = 32
V_DIM = 64
QK_DIM = QK_NOPE + QK_ROPE
ROPE_PAIRS = QK_ROPE // 4
ROPE_THETA = 10000.0
ATTN_SCALE = QK_DIM ** -0.5
LOG2E = 1.4426950408889634

LANES = 128
SUBLANES = 8
BF16_ROWS = 16
VMEM_LIMIT = 56 * 1024 * 1024

HEAD_PAD = LANES
VT_ROWS = V_DIM + BF16_ROWS
ATTN_SUB = 512
ATTN_UNROLL = 2
ATTN_MAX_JUMP = 60.0
FFT_N2 = 128
FFT_PITCH = FFT_N2 + SUBLANES
FFT_BATCH = 16
NEG_BIG = -1e30


def _sigmoid(x):
    return 0.5 * (jnp.tanh(0.5 * x) + 1.0)


def _silu(x):
    return x * _sigmoid(x)


def _cparams(sem):
    return pltpu.CompilerParams(dimension_semantics=sem, vmem_limit_bytes=VMEM_LIMIT)


def _adaln_kernel(cond_ref, w_ref, b_ref, o_ref):
    c = cond_ref[...]
    o_ref[...] = jnp.dot(_silu(c), w_ref[...], preferred_element_type=F32,
                         precision=lax.Precision.HIGHEST) + b_ref[...]


def _adaln(cond, ada_w, ada_b):
    depth, d, n = ada_w.shape
    rows = cond.shape[0]
    tn = 768
    return pl.pallas_call(
        _adaln_kernel,
        grid=(depth, n // tn),
        in_specs=[
            pl.BlockSpec((rows, d), lambda l, j: (0, 0)),
            pl.BlockSpec((None, d, tn), lambda l, j: (l, 0, j)),
            pl.BlockSpec((None, 1, tn), lambda l, j: (l, 0, j)),
        ],
        out_specs=pl.BlockSpec((None, rows, tn), lambda l, j: (l, 0, j)),
        out_shape=jax.ShapeDtypeStruct((depth, rows, n), F32),
        compiler_params=_cparams(("parallel", "parallel")),
        name="adaln",
    )(cond, ada_w, ada_b.reshape(depth, 1, n))


def _in0_kernel(x_ref, mod_ref, w_ref, zl_ref, zf_ref, sg_ref, *, lru_w, fn_w):
    shift = mod_ref[0:1, :]
    scale = mod_ref[1:2, :]
    u = (x_ref[...] * (1.0 + scale) + shift).astype(BF16)
    zl_ref[...] = jnp.dot(u, w_ref[:, :lru_w], preferred_element_type=F32).astype(BF16)
    zf_ref[...] = jnp.dot(u, w_ref[:, lru_w:lru_w + fn_w], preferred_element_type=F32).astype(BF16)
    g0 = lru_w + fn_w
    for c0 in range(0, lru_w + fn_w, 512):
        g = jnp.dot(u, w_ref[:, g0 + c0:g0 + c0 + 512], preferred_element_type=F32)
        sg_ref[:, c0:c0 + 512] = _silu(g).astype(BF16)


def _in0(x, mod, w_bf, tm):
    b_, l_, d = x.shape
    lru_w, fn_w = d, d // 2
    mix = lru_w + fn_w
    return pl.pallas_call(
        functools.partial(_in0_kernel, lru_w=lru_w, fn_w=fn_w),
        grid=(b_, l_ // tm),
        in_specs=[
            pl.BlockSpec((None, tm, d), lambda b, i: (b, i, 0)),
            pl.BlockSpec((None, 3, d), lambda b, i: (b, 0, 0)),
            pl.BlockSpec((d, 2 * mix), lambda b, i: (0, 0)),
        ],
        out_specs=[
            pl.BlockSpec((None, tm, lru_w), lambda b, i: (b, i, 0)),
            pl.BlockSpec((None, tm, fn_w), lambda b, i: (b, i, 0)),
            pl.BlockSpec((None, tm, mix), lambda b, i: (b, i, 0)),
        ],
        out_shape=[
            jax.ShapeDtypeStruct((b_, l_, lru_w), BF16),
            jax.ShapeDtypeStruct((b_, l_, fn_w), BF16),
            jax.ShapeDtypeStruct((b_, l_, mix), BF16),
        ],
        compiler_params=_cparams(("parallel", "parallel")),
        name="in_proj0",
    )(x, mod, w_bf)


def _scan_kernel(z_ref, zp_ref, zn_ref, cw_ref, cb_ref, wg_ref, gb_ref, lam_ref, h0_ref,
                 h_ref, hfin_ref, zz_sc, a_sc, b_sc, hs_sc, carry_sc, *, reverse, t_rows):
    i = pl.program_id(0)
    nt = pl.num_programs(0)
    tt = (nt - 1 - i) if reverse else i
    halo = BF16_ROWS
    nb = z_ref.shape[0]
    width = z_ref.shape[-1]

    @pl.when(i == 0)
    def _():
        carry_sc[...] = jnp.broadcast_to(h0_ref[...], carry_sc.shape)

    nl = -lam_ref[...]
    half_c = (0.5 * LRU_C) * (jnp.maximum(nl, 0.0) + jnp.log(1.0 + jnp.exp(-jnp.abs(nl))))
    half_gb = 0.5 * gb_ref[...]
    n_buf = t_rows + 2 * halo
    for bi in range(nb):
        zz_sc[0:halo, :] = jnp.where(tt > 0, zp_ref[bi].astype(F32), 0.0)
        zz_sc[halo:halo + t_rows, :] = z_ref[bi].astype(F32)
        zz_sc[halo + t_rows:halo + t_rows + halo, :] = jnp.where(tt < nt - 1, zn_ref[bi].astype(F32), 0.0)
        for j in range(width // LRU_GROUP):
            cs = slice(j * LRU_GROUP, (j + 1) * LRU_GROUP)
            zf = zz_sc[:, cs]
            xc = cb_ref[:, cs] + cw_ref[1:2, cs] * zf[halo:halo + t_rows]
            for k, shift in ((0, 1), (2, n_buf - 1), (3, n_buf - 2)):
                xc = xc + cw_ref[k:k + 1, cs] * pltpu.roll(zf, shift, axis=0)[halo:halo + t_rows]
            g = jnp.dot(xc.astype(BF16), wg_ref[j], preferred_element_type=F32)
            t_r = jnp.tanh(g[:, :LRU_GROUP] + half_gb[0:1, cs])
            t_i = jnp.tanh(g[:, LRU_GROUP:] + half_gb[1:2, cs])
            hc = half_c[:, cs]
            nla = hc * t_r + hc
            a = jnp.exp2(nla * (-LOG2E))
            y = jnp.tanh(nla) * (1.0 + a * a)
            sq = jnp.where(y > 0.0, y * lax.rsqrt(y), 0.0)
            hx = 0.5 * xc
            a_sc[bi, :, cs] = a
            b_sc[bi, :, cs] = sq * (hx * t_i + hx)

    def group(gi, hs):
        base = (t_rows - SUBLANES - gi * SUBLANES) if reverse else gi * SUBLANES
        base = pl.multiple_of(base, SUBLANES)
        hs = list(hs)
        for t in range(SUBLANES):
            row = base + ((SUBLANES - 1 - t) if reverse else t)
            for bi in range(nb):
                hs[bi] = a_sc[bi, pl.ds(row, 1), :] * hs[bi] + b_sc[bi, pl.ds(row, 1), :]
                hs_sc[bi, pl.ds(row, 1), :] = hs[bi]
        return tuple(hs)

    h_last = lax.fori_loop(0, t_rows // SUBLANES, group, tuple(carry_sc[bi, 0:1, :] for bi in range(nb)))
    for bi in range(nb):
        carry_sc[bi, 0:1, :] = h_last[bi]
    h_ref[...] = hs_sc[...].astype(BF16)

    @pl.when(i == nt - 1)
    def _():
        for bi in range(nb):
            hfin_ref[bi] = h_last[bi]


def _lru_scan(zl, conv_w, conv_b, wg_d, gb_d, lam_d, h0_d, *, reverse, t_rows):
    b_, l_, w = zl.shape
    nt = l_ // t_rows
    hb = t_rows // BF16_ROWS
    n_hb = l_ // BF16_ROWS

    def tile(i):
        return (nt - 1 - i) if reverse else i

    return pl.pallas_call(
        functools.partial(_scan_kernel, reverse=reverse, t_rows=t_rows),
        grid=(nt,),
        in_specs=[
            pl.BlockSpec((b_, t_rows, w), lambda i: (0, tile(i), 0)),
            pl.BlockSpec((b_, BF16_ROWS, w), lambda i: (0, jnp.maximum(tile(i) * hb - 1, 0), 0)),
            pl.BlockSpec((b_, BF16_ROWS, w), lambda i: (0, jnp.minimum((tile(i) + 1) * hb, n_hb - 1), 0)),
            pl.BlockSpec((CONV_W, w), lambda i: (0, 0)),
            pl.BlockSpec((1, w), lambda i: (0, 0)),
            pl.BlockSpec((w // LRU_GROUP, LRU_GROUP, 2 * LRU_GROUP), lambda i: (0, 0, 0)),
            pl.BlockSpec((2, w), lambda i: (0, 0)),
            pl.BlockSpec((1, w), lambda i: (0, 0)),
            pl.BlockSpec((b_, 1, w), lambda i: (0, 0, 0)),
        ],
        out_specs=[
            pl.BlockSpec((b_, t_rows, w), lambda i: (0, tile(i), 0)),
            pl.BlockSpec((b_, 1, w), lambda i: (0, 0, 0)),
        ],
        out_shape=[
            jax.ShapeDtypeStruct((b_, l_, w), BF16),
            jax.ShapeDtypeStruct((b_, 1, w), F32),
        ],
        scratch_shapes=[
            pltpu.VMEM((t_rows + 2 * BF16_ROWS, w), F32),
            pltpu.VMEM((b_, t_rows, w), F32),
            pltpu.VMEM((b_, t_rows, w), F32),
            pltpu.VMEM((b_, t_rows, w), F32),
            pltpu.VMEM((b_, SUBLANES, w), F32),
        ],
        compiler_params=_cparams(("arbitrary",)),
        name="lru_scan_bwd" if reverse else "lru_scan_fwd",
    )(zl, zl, zl, conv_w, conv_b, wg_d, gb_d, lam_d, h0_d)


def _dft_cos_sin(n):
    k = np.arange(n, dtype=np.int64)
    ang = 2.0 * np.pi * ((k[:, None] * k[None, :]) % n).astype(np.float64) / n
    return np.cos(ang), np.sin(ang)


def _mxu_const(a):
    return jnp.asarray(a, F32).astype(BF16)


def _fnet_lat_kernel(x_ref, fc_ref, m1_ref, c2s2_ref, twa_ref, twb_ref, fw_ref, fb_ref, o_ref, s_sc,
                     *, n1, norm):
    n2 = FFT_N2
    pitch = FFT_PITCH
    gd = x_ref.shape[-1]

    nb = FFT_BATCH

    def stage0(i, carry):
        r0 = pl.multiple_of(i * (nb * n2), nb * n2)
        u = jnp.dot(x_ref[pl.ds(r0, nb * n2), :], fc_ref[...], preferred_element_type=F32)
        for j in range(nb):
            d0 = pl.multiple_of((i * nb + j) * pitch, SUBLANES)
            s_sc[0, pl.ds(d0, n2), :] = u[j * n2:(j + 1) * n2, :gd]
            s_sc[1, pl.ds(d0, n2), :] = u[j * n2:(j + 1) * n2, gd:]
        return carry

    lax.fori_loop(0, n1 // nb, stage0, 0)

    def stage1(a, carry):
        ta_r = twa_ref[0, a]
        ta_i = twa_ref[1, a]
        gs = []
        for b in range(SUBLANES):
            col = a * SUBLANES + b
            g_r = s_sc[0, pl.ds(col, n1, stride=pitch), :]
            g_i = s_sc[1, pl.ds(col, n1, stride=pitch), :]
            gs.append(jnp.concatenate([g_r, g_i], axis=0).astype(BF16))
        y_all = jnp.dot(m1_ref[...], jnp.concatenate(gs, axis=1), preferred_element_type=F32)
        for b in range(SUBLANES):
            col = a * SUBLANES + b
            y_r = y_all[:n1, b * gd:(b + 1) * gd]
            y_i = y_all[n1:, b * gd:(b + 1) * gd]
            tb_r = twb_ref[0, b]
            tb_i = twb_ref[1, b]
            c = ta_r * tb_r - ta_i * tb_i
            s = -(ta_r * tb_i + ta_i * tb_r)
            s_sc[0, pl.ds(col, n1, stride=pitch), :] = y_r * c + y_i * s
            s_sc[1, pl.ds(col, n1, stride=pitch), :] = y_i * c - y_r * s
        return carry

    lax.fori_loop(0, n2 // SUBLANES, stage1, 0)

    def stage2(i, carry):
        zs = []
        for j in range(nb):
            d0 = pl.multiple_of((i * nb + j) * pitch, SUBLANES)
            zs.append(jnp.concatenate([s_sc[0, pl.ds(d0, n2), :], s_sc[1, pl.ds(d0, n2), :]],
                                      axis=0).astype(BF16))
        z = jnp.concatenate(zs, axis=1)
        xr = jnp.dot(c2s2_ref[...], z, preferred_element_type=F32) * norm
        xs = jnp.concatenate([xr[:, j * gd:(j + 1) * gd] for j in range(nb)], axis=0).astype(BF16)
        out = jnp.dot(xs, fw_ref[...], preferred_element_type=F32) + fb_ref[...]
        for j in range(nb):
            o_ref[pl.ds(i * nb + j, n2, stride=n1), :] = out[j * n2:(j + 1) * n2, :]
        return carry

    lax.fori_loop(0, n1 // nb, stage2, 0)


def _fnet_lat(zf, fnet_w_bf, fnet_b):
    b_, l_, fw = zf.shape
    gd = FNET_GROUP_DIM
    groups = fw // gd
    n2 = FFT_N2
    n1 = l_ // n2
    assert n1 * n2 == l_ and n1 % SUBLANES == 0
    cc, sc = _dft_cos_sin(gd)
    fc = _mxu_const(np.concatenate([cc, -sc], axis=1))
    c1, s1 = _dft_cos_sin(n1)
    m1 = _mxu_const(np.block([[c1, s1], [-s1, c1]]))
    c2, s2 = _dft_cos_sin(n2)
    c2s2 = _mxu_const(np.concatenate([c2, s2], axis=1))
    k1 = np.arange(n1, dtype=np.float64)[None, :, None]
    ang_a = 2.0 * np.pi * (np.arange(n2 // SUBLANES, dtype=np.float64) * SUBLANES)[:, None, None] * k1 / l_
    ang_b = 2.0 * np.pi * np.arange(SUBLANES, dtype=np.float64)[:, None, None] * k1 / l_
    ones = np.ones((1, 1, gd))
    twa = jnp.asarray(np.stack([np.cos(ang_a) * ones, -np.sin(ang_a) * ones]), F32)
    twb = jnp.asarray(np.stack([np.cos(ang_b) * ones, -np.sin(ang_b) * ones]), F32)
    norm = 1.0 / math.sqrt(l_ * gd)
    const2 = lambda b, g: (0, 0)
    const4 = lambda b, g: (0, 0, 0, 0)
    return pl.pallas_call(
        functools.partial(_fnet_lat_kernel, n1=n1, norm=norm),
        grid=(b_, groups),
        in_specs=[
            pl.BlockSpec((None, l_, gd), lambda b, g: (b, 0, g)),
            pl.BlockSpec(fc.shape, const2),
            pl.BlockSpec(m1.shape, const2),
            pl.BlockSpec(c2s2.shape, const2),
            pl.BlockSpec(twa.shape, const4),
            pl.BlockSpec(twb.shape, const4),
            pl.BlockSpec((None, gd, gd), lambda b, g: (g, 0, 0)),
            pl.BlockSpec((None, 1, gd), lambda b, g: (g, 0, 0)),
        ],
        out_specs=pl.BlockSpec((None, l_, gd), lambda b, g: (b, 0, g)),
        out_shape=jax.ShapeDtypeStruct((b_, l_, fw), F32),
        scratch_shapes=[pltpu.VMEM((2, n1 * FFT_PITCH, gd), F32)],
        compiler_params=_cparams(("parallel", "parallel")),
        name="fnet_lat",
    )(zf, fc, m1, c2s2, twa, twb, fnet_w_bf, fnet_b.reshape(groups, 1, gd))


def _fnet_ctx_kernel(x_ref, fc_ref, cs_ref, fw_ref, fb_ref, o_ref, *, norm):
    gd = x_ref.shape[-1]
    u = jnp.dot(x_ref[...], fc_ref[...], preferred_element_type=F32)
    z = jnp.concatenate([u[:, :gd], u[:, gd:]], axis=0).astype(BF16)
    xr = jnp.dot(cs_ref[...], z, preferred_element_type=F32) * norm
    o_ref[...] = jnp.dot(xr.astype(BF16), fw_ref[...], preferred_element_type=F32) + fb_ref[...]


def _fnet_ctx(zf, fnet_w_bf, fnet_b):
    b_, l_, fw = zf.shape
    gd = FNET_GROUP_DIM
    groups = fw // gd
    cc, sc = _dft_cos_sin(gd)
    fc = _mxu_const(np.concatenate([cc, -sc], axis=1))
    cl, sl = _dft_cos_sin(l_)
    cs = _mxu_const(np.concatenate([cl, sl], axis=1))
    norm = 1.0 / math.sqrt(l_ * gd)
    const2 = lambda b, g: (0, 0)
    return pl.pallas_call(
        functools.partial(_fnet_ctx_kernel, norm=norm),
        grid=(b_, groups),
        in_specs=[
            pl.BlockSpec((None, l_, gd), lambda b, g: (b, 0, g)),
            pl.BlockSpec(fc.shape, const2),
            pl.BlockSpec(cs.shape, const2),
            pl.BlockSpec((None, gd, gd), lambda b, g: (g, 0, 0)),
            pl.BlockSpec((None, 1, gd), lambda b, g: (g, 0, 0)),
        ],
        out_specs=pl.BlockSpec((None, l_, gd), lambda b, g: (b, 0, g)),
        out_shape=jax.ShapeDtypeStruct((b_, l_, fw), F32),
        compiler_params=_cparams(("parallel", "parallel")),
        name="fnet_ctx",
    )(zf, fc, cs, fnet_w_bf, fnet_b.reshape(groups, 1, gd))


def _deepnorm(resid, gate, y, g, b):
    v = DEEPNORM_ALPHA * resid + gate * y
    mu = jnp.mean(v, axis=-1, keepdims=True)
    vc = v - mu
    var = jnp.mean(vc * vc, axis=-1, keepdims=True)
    return vc * lax.rsqrt(var + LN_EPS) * g + b


def _out0_kernel(hf_ref, hb_ref, f_ref, sg_ref, x_ref, mod_ref, w_ref, lg_ref, lb_ref, o_ref, *, lru_w):
    r = hf_ref[...].astype(F32) + hb_ref[...].astype(F32)
    m_l = (r * sg_ref[:, :lru_w].astype(F32)).astype(BF16)
    m_f = (f_ref[...] * sg_ref[:, lru_w:].astype(F32)).astype(BF16)
    y = jnp.dot(m_l, w_ref[:lru_w, :], preferred_element_type=F32)
    y = y + jnp.dot(m_f, w_ref[lru_w:, :], preferred_element_type=F32)
    o_ref[...] = _deepnorm(x_ref[...], mod_ref[2:3, :], y, lg_ref[...], lb_ref[...])


def _out0(hf, hb, f, sg, x, mod, w_bf, ln_g, ln_b, tm):
    b_, l_, d = x.shape
    lru_w = hf.shape[-1]
    fn_w = f.shape[-1]
    mix = lru_w + fn_w
    row = lambda b, i: (b, i, 0)
    const2 = lambda b, i: (0, 0)
    return pl.pallas_call(
        functools.partial(_out0_kernel, lru_w=lru_w),
        grid=(b_, l_ // tm),
        in_specs=[
            pl.BlockSpec((None, tm, lru_w), row),
            pl.BlockSpec((None, tm, lru_w), row),
            pl.BlockSpec((None, tm, fn_w), row),
            pl.BlockSpec((None, tm, mix), row),
            pl.BlockSpec((None, tm, d), row),
            pl.BlockSpec((None, 3, d), lambda b, i: (b, 0, 0)),
            pl.BlockSpec((mix, d), const2),
            pl.BlockSpec((1, d), const2),
            pl.BlockSpec((1, d), const2),
        ],
        out_specs=pl.BlockSpec((None, tm, d), row),
        out_shape=jax.ShapeDtypeStruct((b_, l_, d), F32),
        compiler_params=_cparams(("parallel", "parallel")),
        name="out_proj0",
    )(hf, hb, f, sg, x, mod, w_bf, ln_g, ln_b)


def _out1_kernel(o_ref_in, sg_ref, x_ref, mod_ref, w_ref, lg_ref, lb_ref, o_ref):
    m = (o_ref_in[...].astype(F32) * sg_ref[...].astype(F32)).astype(BF16)
    y = jnp.dot(m, w_ref[...], preferred_element_type=F32)
    o_ref[...] = _deepnorm(x_ref[...], mod_ref[2:3, :], y, lg_ref[...], lb_ref[...])


def _out1(o, sg, x, mod, w_bf, ln_g, ln_b, tm):
    b_, l_, d = x.shape
    wdt = o.shape[-1]
    row = lambda b, i: (b, i, 0)
    const2 = lambda b, i: (0, 0)
    return pl.pallas_call(
        _out1_kernel,
        grid=(b_, l_ // tm),
        in_specs=[
            pl.BlockSpec((None, tm, wdt), row),
            pl.BlockSpec((None, tm, wdt), row),
            pl.BlockSpec((None, tm, d), row),
            pl.BlockSpec((None, 3, d), lambda b, i: (b, 0, 0)),
            pl.BlockSpec((wdt, d), const2),
            pl.BlockSpec((1, d), const2),
            pl.BlockSpec((1, d), const2),
        ],
        out_specs=pl.BlockSpec((None, tm, d), row),
        out_shape=jax.ShapeDtypeStruct((b_, l_, d), F32),
        compiler_params=_cparams(("parallel", "parallel")),
        name="out_proj1",
    )(o, sg, x, mod, w_bf, ln_g, ln_b)


def _rms(x, g):
    return x * lax.rsqrt(jnp.mean(x * x, axis=-1, keepdims=True) + RMS_EPS) * g


def _in1_kernel(*refs, rope, want_q):
    if want_q:
        (x_ref, mod_ref, w_ref, qg_ref, kg_ref, wq_ref, wqs_ref, wk_ref, wvt_ref, cos_ref, sin_ref,
         q_ref, k_ref, vt_ref, sg_ref) = refs
    else:
        (x_ref, mod_ref, w_ref, kg_ref, wk_ref, wvt_ref, k_ref, vt_ref) = refs
    shift = mod_ref[0:1, :]
    scale = mod_ref[1:2, :]
    u = (x_ref[...] * (1.0 + scale) + shift).astype(BF16)
    hp = HEAD_PAD
    o_kv = Q_LORA
    o_kr = Q_LORA + KV_LORA
    o_g = o_kr + 2 * hp

    kvn = _rms(jnp.dot(u, w_ref[:, o_kv:o_kv + KV_LORA], preferred_element_type=F32), kg_ref[...]).astype(BF16)
    kr = jnp.dot(u, w_ref[:, o_kr:o_kr + 2 * hp], preferred_element_type=F32)
    if rope:
        cos = cos_ref[...]
        sin = sin_ref[...]
        krope = kr[:, :hp] * cos + kr[:, hp:] * sin
    else:
        krope = kr[:, :hp]
    for h2 in range(MLA_HEADS // 2):
        cs = slice(2 * h2 * hp, (2 * h2 + 2) * hp)
        kn = jnp.dot(kvn, wk_ref[:, cs], preferred_element_type=F32)
        k_ref[2 * h2] = (kn[:, :hp] + krope).astype(BF16)
        k_ref[2 * h2 + 1] = (kn[:, hp:] + krope).astype(BF16)
    vt = lax.dot_general(wvt_ref[...], kvn, (((1,), (1,)), ((), ())), preferred_element_type=F32)
    row = lax.broadcasted_iota(jnp.int32, (vt.shape[0], 1), 0)
    vt_ref[...] = (vt + (row % VT_ROWS == V_DIM).astype(F32)).astype(BF16)

    if want_q:
        qn = _rms(jnp.dot(u, w_ref[:, :Q_LORA], preferred_element_type=F32), qg_ref[...]).astype(BF16)
        qscale = ATTN_SCALE * LOG2E
        cq = cos * qscale
        sq = sin * qscale
        for h2 in range(MLA_HEADS // 2):
            cs = slice(2 * h2 * hp, (2 * h2 + 2) * hp)
            z1 = jnp.dot(qn, wq_ref[:, cs], preferred_element_type=F32)
            z2 = jnp.dot(qn, wqs_ref[:, cs], preferred_element_type=F32)
            q_ref[2 * h2] = (z1[:, :hp] * cq + z2[:, :hp] * sq).astype(BF16)
            q_ref[2 * h2 + 1] = (z1[:, hp:] * cq + z2[:, hp:] * sq).astype(BF16)
        gw = sg_ref.shape[-1]
        for c0 in range(0, gw, 512):
            g = jnp.dot(u, w_ref[:, o_g + c0:o_g + c0 + 512], preferred_element_type=F32)
            sg_ref[:, c0:c0 + 512] = _silu(g).astype(BF16)


def _in1(x, mod, w_in_p, qg, kg, wq_p, wqs_p, wk_p, wvt_p, cos_t, sin_t, tm, *, want_q):
    b_, l_, d = x.shape
    hp = HEAD_PAD
    nh = MLA_HEADS
    gw = nh * V_DIM
    row = lambda b, i: (b, i, 0)
    const2 = lambda b, i: (0, 0)
    head = lambda b, i: (b, 0, i, 0)
    kv_specs = [pl.BlockSpec((None, nh, tm, hp), head),
                pl.BlockSpec((None, nh * VT_ROWS, tm), lambda b, i: (b, 0, i))]
    kv_shapes = [jax.ShapeDtypeStruct((b_, nh, l_, hp), BF16),
                 jax.ShapeDtypeStruct((b_, nh * VT_ROWS, l_), BF16)]
    if want_q:
        ins = [x, mod, w_in_p, qg, kg, wq_p, wqs_p, wk_p, wvt_p, cos_t, sin_t]
        in_specs = [
            pl.BlockSpec((None, tm, d), row),
            pl.BlockSpec((None, 3, d), lambda b, i: (b, 0, 0)),
            pl.BlockSpec(w_in_p.shape, const2),
            pl.BlockSpec(qg.shape, const2),
            pl.BlockSpec(kg.shape, const2),
            pl.BlockSpec(wq_p.shape, const2),
            pl.BlockSpec(wqs_p.shape, const2),
            pl.BlockSpec(wk_p.shape, const2),
            pl.BlockSpec(wvt_p.shape, const2),
            pl.BlockSpec((tm, hp), lambda b, i: (i, 0)),
            pl.BlockSpec((tm, hp), lambda b, i: (i, 0)),
        ]
        out_specs = [pl.BlockSpec((None, nh, tm, hp), head)] + kv_specs + [pl.BlockSpec((None, tm, gw), row)]
        out_shape = [jax.ShapeDtypeStruct((b_, nh, l_, hp), BF16)] + kv_shapes + [
            jax.ShapeDtypeStruct((b_, l_, gw), BF16)]
    else:
        ins = [x, mod, w_in_p, kg, wk_p, wvt_p]
        in_specs = [
            pl.BlockSpec((None, tm, d), row),
            pl.BlockSpec((None, 3, d), lambda b, i: (b, 0, 0)),
            pl.BlockSpec(w_in_p.shape, const2),
            pl.BlockSpec(kg.shape, const2),
            pl.BlockSpec(wk_p.shape, const2),
            pl.BlockSpec(wvt_p.shape, const2),
        ]
        out_specs = kv_specs
        out_shape = kv_shapes
    return pl.pallas_call(
        functools.partial(_in1_kernel, rope=want_q, want_q=want_q),
        grid=(b_, l_ // tm),
        in_specs=in_specs,
        out_specs=out_specs,
        out_shape=out_shape,
        compiler_params=_cparams(("parallel", "parallel")),
        name="in_proj1" if want_q else "in_proj1_ctx",
    )(*ins)


def _attn_kernel(q_ref, kc_ref, vtc_ref, k_ref, vt_ref, o_ref, m_sc, acc_sc, p_sc, ref_sc, racc_sc, jump_sc):
    sub = ATTN_SUB
    n_sub = k_ref.shape[1] // sub

    def qk(k, hh):
        return lax.dot_general(k, q_ref[hh], (((1,), (1,)), ((), ())), preferred_element_type=F32)

    def update(hh, s, vt):
        m_prev = m_sc[hh]
        m_new = jnp.maximum(m_prev, jnp.max(s, axis=0, keepdims=True))
        alpha = jnp.exp2(m_prev - m_new)
        p = jnp.exp2(s - m_new).astype(BF16)
        acc_sc[hh] = alpha * acc_sc[hh] + jnp.dot(vt, p, preferred_element_type=F32)
        m_sc[hh] = m_new
        racc_sc[hh] = m_new

    def start():
        m_sc[...] = jnp.full(m_sc.shape, NEG_BIG, F32)
        acc_sc[...] = jnp.zeros(acc_sc.shape, F32)
        jump_sc[...] = jnp.zeros(jump_sc.shape, F32)
        for hh in range(2):
            update(hh, qk(kc_ref[hh], hh), vtc_ref[hh * VT_ROWS:(hh + 1) * VT_ROWS, :])

    def k_at(j, hh):
        return k_ref[hh, pl.ds(pl.multiple_of(j * sub, sub), sub), :]

    def vt_at(j, hh):
        return vt_ref[hh * VT_ROWS:(hh + 1) * VT_ROWS, pl.ds(pl.multiple_of(j * sub, sub), sub)]

    def scores(slot, j):
        for hh in range(2):
            s = qk(k_at(j, hh), hh)
            r = m_sc[hh]
            p_sc[slot, hh] = jnp.exp2(s - r).astype(BF16)
            ref_sc[slot, hh] = r
            m_new = jnp.maximum(r, jnp.max(s, axis=0, keepdims=True))
            jump_sc[hh] = jnp.maximum(jump_sc[hh], m_new - r)
            m_sc[hh] = m_new

    def values(slot, j):
        for hh in range(2):
            r_p = ref_sc[slot, hh]
            alpha = jnp.exp2(racc_sc[hh] - r_p)
            acc_sc[hh] = alpha * acc_sc[hh] + jnp.dot(vt_at(j, hh), p_sc[slot, hh], preferred_element_type=F32)
            racc_sc[hh] = r_p

    start()
    scores(0, 0)

    def body(i, carry):
        j = 2 * i
        scores(1, j + 1)
        values(0, j)
        scores(0, jnp.minimum(j + 2, n_sub - 1))
        values(1, j + 1)
        return carry

    lax.fori_loop(0, n_sub // 2, body, 0, unroll=math.gcd(ATTN_UNROLL, n_sub // 2))

    @pl.when(jnp.max(jump_sc[...]) > ATTN_MAX_JUMP)
    def _():
        start()

        def safe_body(j, carry):
            for hh in range(2):
                update(hh, qk(k_at(j, hh), hh), vt_at(j, hh))
            return carry

        lax.fori_loop(0, n_sub, safe_body, 0)

    halves = []
    for hh in range(2):
        acc = acc_sc[hh]
        halves.append(acc[:V_DIM] * (1.0 / acc[V_DIM:V_DIM + 1]))
    o_ref[...] = jnp.concatenate(halves, axis=0).T.astype(BF16)


def _attention(q, k_ctx, vt_ctx, k_lat, vt_lat, tq):
    b_, nh, s_, hp = q.shape
    c_ = k_ctx.shape[2]
    assert s_ % (2 * ATTN_SUB) == 0
    return pl.pallas_call(
        _attn_kernel,
        grid=(b_, nh // 2, s_ // tq),
        in_specs=[
            pl.BlockSpec((None, 2, tq, hp), lambda b, h, i: (b, h, i, 0)),
            pl.BlockSpec((None, 2, c_, hp), lambda b, h, i: (b, h, 0, 0)),
            pl.BlockSpec((None, 2 * VT_ROWS, c_), lambda b, h, i: (b, h, 0)),
            pl.BlockSpec((None, 2, s_, hp), lambda b, h, i: (b, h, 0, 0), pipeline_mode=pl.Buffered(1)),
            pl.BlockSpec((None, 2 * VT_ROWS, s_), lambda b, h, i: (b, h, 0), pipeline_mode=pl.Buffered(1)),
        ],
        out_specs=pl.BlockSpec((None, tq, 2 * V_DIM), lambda b, h, i: (b, i, h)),
        out_shape=jax.ShapeDtypeStruct((b_, s_, nh * V_DIM), BF16),
        scratch_shapes=[pltpu.VMEM((2, 1, tq), F32), pltpu.VMEM((2, VT_ROWS, tq), F32),
                        pltpu.VMEM((2, 2, ATTN_SUB, tq), BF16), pltpu.VMEM((2, 2, 1, tq), F32),
                        pltpu.VMEM((2, 1, tq), F32), pltpu.VMEM((2, 1, tq), F32)],
        compiler_params=_cparams(("parallel", "parallel", "arbitrary")),
        name="mla_attention",
    )(q, k_ctx, vt_ctx, k_lat, vt_lat)


def _gate_blocks(gate_w_d):
    hd = LRU_HEAD_DIM
    per = LRU_GROUP // hd
    ng = LRU_HEADS // per
    w = gate_w_d.reshape(2, ng, per, hd, hd)
    eye = jnp.eye(per, dtype=gate_w_d.dtype)
    blk = jnp.einsum('gnpij,pq->gnpiqj', w, eye).reshape(2, ng, LRU_GROUP, LRU_GROUP)
    return (0.5 * jnp.concatenate([blk[0], blk[1]], axis=-1)).astype(BF16)


def _pad_heads(w, per_head, offset=0):
    k = w.shape[0]
    w3 = w.reshape(k, MLA_HEADS, per_head)
    out = jnp.zeros((k, MLA_HEADS, HEAD_PAD), w.dtype)
    out = out.at[:, :, offset:offset + per_head].set(w3)
    return out.reshape(k, MLA_HEADS * HEAD_PAD)


def _rope_swap_idx():
    return np.arange(QK_ROPE) ^ ROPE_PAIRS


def _prep_mla(w_in, w_uq, w_ukv):
    d = w_in.shape[0]
    qc = w_in[:, :Q_LORA]
    kvc = w_in[:, Q_LORA:Q_LORA + KV_LORA]
    kr = w_in[:, Q_LORA + KV_LORA:Q_LORA + KV_LORA + QK_ROPE]
    g = w_in[:, Q_LORA + KV_LORA + QK_ROPE:]
    swap = _rope_swap_idx()
    kr_blk = jnp.zeros((d, HEAD_PAD), w_in.dtype).at[:, QK_NOPE:QK_NOPE + QK_ROPE].set(kr)
    krs_blk = jnp.zeros((d, HEAD_PAD), w_in.dtype).at[:, QK_NOPE:QK_NOPE + QK_ROPE].set(kr[:, swap])
    w_in_p = jnp.concatenate([qc, kvc, kr_blk, krs_blk, g], axis=1).astype(BF16)

    wq3 = w_uq.reshape(Q_LORA, MLA_HEADS, QK_DIM)
    wq_p = _pad_heads(w_uq, QK_DIM).astype(BF16)
    rope_sw = wq3[:, :, QK_NOPE:][:, :, swap]
    wqs = jnp.zeros((Q_LORA, MLA_HEADS, HEAD_PAD), w_uq.dtype).at[:, :, QK_NOPE:QK_NOPE + QK_ROPE].set(rope_sw)
    wqs_p = wqs.reshape(Q_LORA, MLA_HEADS * HEAD_PAD).astype(BF16)

    wkv3 = w_ukv.reshape(KV_LORA, MLA_HEADS, QK_NOPE + V_DIM)
    wk_p = _pad_heads(wkv3[:, :, :QK_NOPE].reshape(KV_LORA, -1), QK_NOPE).astype(BF16)
    wv3 = jnp.transpose(wkv3[:, :, QK_NOPE:], (1, 2, 0))
    wvt_p = jnp.pad(wv3, ((0, 0), (0, VT_ROWS - V_DIM), (0, 0))).reshape(MLA_HEADS * VT_ROWS, KV_LORA)
    return w_in_p, wq_p, wqs_p, wk_p, wvt_p.astype(BF16)


def _rope_tables(s_):
    rows = s_ // GRID_W
    f32 = np.float32
    inv = f32(ROPE_THETA) ** (-np.arange(ROPE_PAIRS, dtype=f32) / f32(ROPE_PAIRS))
    a_r = np.arange(rows, dtype=f32)[:, None] * inv
    a_c = np.arange(GRID_W, dtype=f32)[:, None] * inv
    shape = (rows, GRID_W, ROPE_PAIRS)
    by_row = lambda t: np.broadcast_to(t[:, None, :], shape)
    by_col = lambda t: np.broadcast_to(t[None, :, :], shape)
    cr, sr, cc, sc = by_row(np.cos(a_r)), by_row(np.sin(a_r)), by_col(np.cos(a_c)), by_col(np.sin(a_c))
    pad = HEAD_PAD - QK_NOPE - QK_ROPE
    cos_t = np.concatenate([np.ones(shape[:2] + (QK_NOPE,), f32), cr, cr, cc, cc,
                            np.ones(shape[:2] + (pad,), f32)], axis=-1)
    sin_t = np.concatenate([np.zeros(shape[:2] + (QK_NOPE,), f32), -sr, sr, -sc, sc,
                            np.zeros(shape[:2] + (pad,), f32)], axis=-1)
    return jnp.asarray(cos_t.reshape(s_, HEAD_PAD), F32), jnp.asarray(sin_t.reshape(s_, HEAD_PAD), F32)


def kernel(x, c, ctx, c_ctx, ada_w, ada_b, ln_g, ln_b, w_in_rf, conv_w, conv_b, lru_gate_w, lru_gate_b,
           lru_lambda, fnet_w, fnet_b, w_out_rf, w_in_mla, q_norm_g, kv_norm_g, w_uq, w_ukv, w_out_mla):
    b_, s_, d = x.shape
    c_len = ctx.shape[1]
    tm = 1024 if s_ % 1024 == 0 else s_
    t_scan = 512 if s_ % 512 == 0 else s_

    cond = jnp.zeros((SUBLANES, d), F32).at[:b_].set(c).at[b_].set(c_ctx)
    mod = _adaln(cond, ada_w, ada_b).reshape(DEPTH, SUBLANES, 3, d)
    mod_lat = [mod[l, :b_] for l in range(DEPTH)]
    mod_ctx = [jnp.broadcast_to(mod[l, b_][None], (b_, 3, d)) for l in range(DEPTH)]

    w_in0 = w_in_rf[0].astype(BF16)
    w_out0 = w_out_rf[0].astype(BF16)
    fw_bf = fnet_w[0].astype(BF16)
    wg = [_gate_blocks(lru_gate_w[0, dd]) for dd in range(2)]
    cb = conv_b[0].reshape(1, -1)

    zl_c, zf_c, sg_c = _in0(ctx, mod_ctx[0], w_in0, c_len)
    zl_l, zf_l, sg_l = _in0(x, mod_lat[0], w_in0, tm)

    h_c, h_l = [], []
    for dd in range(2):
        rev = dd == 1
        lam = lru_lambda[0, dd].reshape(1, -1)
        zero = jnp.zeros((b_, 1, d), F32)
        hc, hc_fin = _lru_scan(zl_c, conv_w[0], cb, wg[dd], lru_gate_b[0, dd], lam, zero,
                               reverse=rev, t_rows=c_len)
        hl, _ = _lru_scan(zl_l, conv_w[0], cb, wg[dd], lru_gate_b[0, dd], lam, hc_fin,
                          reverse=rev, t_rows=t_scan)
        h_c.append(hc)
        h_l.append(hl)

    f_c = _fnet_ctx(zf_c, fw_bf, fnet_b[0])
    f_l = _fnet_lat(zf_l, fw_bf, fnet_b[0])

    lg0, lb0 = ln_g[0].reshape(1, d), ln_b[0].reshape(1, d)
    h1_c = _out0(h_c[0], h_c[1], f_c, sg_c, ctx, mod_ctx[0], w_out0, lg0, lb0, c_len)
    h1_l = _out0(h_l[0], h_l[1], f_l, sg_l, x, mod_lat[0], w_out0, lg0, lb0, tm)

    w_in_p, wq_p, wqs_p, wk_p, wvt_p = _prep_mla(w_in_mla[0], w_uq[0], w_ukv[0])
    qg = q_norm_g[0].reshape(1, -1)
    kg = kv_norm_g[0].reshape(1, -1)
    cos_t, sin_t = _rope_tables(s_)

    k_c, v_c = _in1(h1_c, mod_ctx[1], w_in_p, qg, kg, wq_p, wqs_p, wk_p, wvt_p, cos_t, sin_t, c_len,
                    want_q=False)
    q_l, k_l, v_l, sg1 = _in1(h1_l, mod_lat[1], w_in_p, qg, kg, wq_p, wqs_p, wk_p, wvt_p, cos_t, sin_t, tm,
                              want_q=True)

    tq = 4096 if s_ % 4096 == 0 else s_
    o = _attention(q_l, k_c, v_c, k_l, v_l, tq)

    lg1, lb1 = ln_g[1].reshape(1, d), ln_b[1].reshape(1, d)
    return _out1(o, sg1, h1_l, mod_lat[1], w_out_mla[0].astype(BF16), lg1, lb1, tm)
```
